```python
import jax, jax.numpy as jnp
from jax import lax
import numpy as np

D_MODEL = 4096
BATCH = 4
SEQ = 2048
DEPTH = 4
DEC_BATCH = 8
DEC_SEQ = 8
PAST_LEN = 8192
PAGE_SIZE = 128

HEAD_DIM = 128
H_SB = D_MODEL // (2 * HEAD_DIM)
H_DSA = D_MODEL // (2 * HEAD_DIM)
H_IDX = D_MODEL // 128
D_IDX = 64
TOPK_MAX = 256
D_FF = 4 * D_MODEL
Q_BLOCK = 128
ROPE_THETA = 10000.0
EPS = 1e-6
SB_W = H_SB * HEAD_DIM
DSA_W = H_DSA * HEAD_DIM
IN_WIDTHS = (SB_W, SB_W, SB_W, DSA_W, DSA_W, DSA_W, H_IDX * D_IDX, D_IDX, H_IDX, D_MODEL, D_MODEL)
IN_SPLITS = [int(s) for s in np.cumsum(IN_WIDTHS)[:-1]]
N_IN = sum(IN_WIDTHS)

kernel_name = 'stickbreak_dsa_gated_hybrid_step'


def rms_norm(x, g):
    xf = x.astype(jnp.float32)
    y = xf * lax.rsqrt(jnp.mean(xf * xf, axis=-1, keepdims=True) + EPS)
    return (y * g.astype(jnp.float32)).astype(x.dtype)


def rope(x, pos):
    d = x.shape[-1]
    inv = ROPE_THETA ** (-jnp.arange(0, d, 2, dtype=jnp.float32) / d)
    ang = pos.astype(jnp.float32)[:, None] * inv[None, :]
    cos = jnp.cos(ang)[None, :, None, :]
    sin = jnp.sin(ang)[None, :, None, :]
    xf = x.astype(jnp.float32)
    x1, x2 = xf[..., : d // 2], xf[..., d // 2:]
    return jnp.concatenate([x1 * cos - x2 * sin, x2 * cos + x1 * sin], axis=-1).astype(x.dtype)


def map_query_blocks(fn, q_inputs, q_pos):
    B, T = q_inputs[0].shape[:2]
    bq = Q_BLOCK if T % Q_BLOCK == 0 else T
    nb = T // bq
    blocks = tuple(jnp.moveaxis(a.reshape((B, nb, bq) + a.shape[2:]), 1, 0) for a in q_inputs)
    out = lax.map(lambda args: fn(*args), blocks + (q_pos.reshape(nb, bq),))
    out = jnp.moveaxis(out, 0, 1)
    return out.reshape((B, T) + out.shape[3:])


def stick_breaking_attend(q, qpos, k, v, kpos):
    z = jnp.einsum('bqhd,bshd->bhqs', q, k).astype(jnp.float32) * (HEAD_DIM ** -0.5)
    allowed = (kpos[None, :] < qpos[:, None])[None, None]
    log_keep = jnp.where(allowed, jax.nn.log_sigmoid(-z), 0.0)
    after = lax.cumsum(log_keep, axis=3, reverse=True) - log_keep
    weights = jnp.where(allowed, jnp.exp(jax.nn.log_sigmoid(z) + after), 0.0)
    return jnp.einsum('bhqs,bshd->bqhd', weights.astype(v.dtype), v)


def dsa_attend(q, qi, wi, qpos, k, v, ki, kpos, n_sel):
    causal = (kpos[None, :] <= qpos[:, None])[None]
    rel = jax.nn.relu(jnp.einsum('bqid,bsd->bqis', qi, ki).astype(jnp.float32) * (D_IDX ** -0.5))
    score = jnp.einsum('bqi,bqis->bqs', wi.astype(jnp.float32), rel)
    score = jnp.where(causal, score, -jnp.inf)
    _, idx = lax.top_k(score, n_sel)
    valid = kpos[idx] <= qpos[None, :, None]
    k_sel = jax.vmap(lambda kk, ii: kk[ii])(k, idx)
    v_sel = jax.vmap(lambda vv, ii: vv[ii])(v, idx)
    logits = jnp.einsum('bqhd,bqnhd->bhqn', q, k_sel).astype(jnp.float32) * (HEAD_DIM ** -0.5)
    logits = jnp.where(valid[:, None], logits, -jnp.inf)
    p = jax.nn.softmax(logits, axis=-1)
    return jnp.einsum('bhqn,bqnhd->bqhd', p.astype(v.dtype), v_sel)


def token_mix(h, pos, w_in, w_br_sb, w_br_dsa, w_out, past=None):
    B, T, _ = h.shape
    q_sb, k_sb, v_sb, q_ds, k_ds, v_ds, q_ix, k_ix, w_ix, g_sb, g_ds = jnp.split(h @ w_in, IN_SPLITS, axis=-1)
    q_sb = q_sb.reshape(B, T, H_SB, HEAD_DIM)
    k_sb = k_sb.reshape(B, T, H_SB, HEAD_DIM)
    v_sb = v_sb.reshape(B, T, H_SB, HEAD_DIM)
    q_ds = rope(q_ds.reshape(B, T, H_DSA, HEAD_DIM), pos)
    k_ds = rope(k_ds.reshape(B, T, H_DSA, HEAD_DIM), pos)
    v_ds = v_ds.reshape(B, T, H_DSA, HEAD_DIM)
    q_ix = rope(q_ix.reshape(B, T, H_IDX, D_IDX), pos)
    k_ix = rope(k_ix[:, :, None, :], pos)[:, :, 0, :]
    w_ix = w_ix * (H_IDX ** -0.5)
    new_rows = (k_sb, v_sb, k_ds, v_ds, k_ix)
    if past is None:
        ka, va, kd, vd, ki = new_rows
        kpos = pos
    else:
        ka, va, kd, vd, ki = (jnp.concatenate([p, n], axis=1) for p, n in zip(past, new_rows))
        kpos = jnp.concatenate([jnp.arange(past[0].shape[1], dtype=jnp.int32), pos])
    n_sel = min(TOPK_MAX, kd.shape[1] // 4)
    o_sb = map_query_blocks(lambda q, qp: stick_breaking_attend(q, qp, ka, va, kpos), (q_sb,), pos)
    o_ds = map_query_blocks(lambda q, qi, wi, qp: dsa_attend(q, qi, wi, qp, kd, vd, ki, kpos, n_sel),
                            (q_ds, q_ix, w_ix), pos)
    merged = (jax.nn.sigmoid(g_sb) * (o_sb.reshape(B, T, SB_W) @ w_br_sb)
              + jax.nn.sigmoid(g_ds) * (o_ds.reshape(B, T, DSA_W) @ w_br_dsa))
    return merged @ w_out, new_rows


def sq_relu_mlp(x, w_up, w_down):
    return jnp.square(jax.nn.relu(x @ w_up)) @ w_down


def gather_pages(cache_l, page_table):
    g = cache_l[page_table]
    return g.reshape((g.shape[0], g.shape[1] * g.shape[2]) + g.shape[3:])


def setup_inputs(seed: int = 0) -> dict:
    key = jax.random.key(seed)
    ks = jax.random.split(key, 20)
    f32 = jnp.float32

    def nrm(k, shape, scale=1.0):
        return jax.random.normal(k, shape, f32) * scale

    n_pages = PAST_LEN // PAGE_SIZE
    n_used = DEC_BATCH * n_pages
    n_pool = n_used + (n_used + 3) // 4
    page_table = jax.random.permutation(ks[7], n_pool)[:n_used].reshape(DEC_BATCH, n_pages).astype(jnp.int32)
    return {
        'x_prompt': nrm(ks[0], (BATCH, SEQ, D_MODEL)),
        'x_sample': nrm(ks[1], (DEC_BATCH, DEC_SEQ, D_MODEL)),
        'cache_sb_k': nrm(ks[2], (DEPTH, n_pool, PAGE_SIZE, H_SB, HEAD_DIM)),
        'cache_sb_v': nrm(ks[3], (DEPTH, n_pool, PAGE_SIZE, H_SB, HEAD_DIM)),
        'cache_dsa_k': nrm(ks[4], (DEPTH, n_pool, PAGE_SIZE, H_DSA, HEAD_DIM)),
        'cache_dsa_v': nrm(ks[5], (DEPTH, n_pool, PAGE_SIZE, H_DSA, HEAD_DIM)),
        'cache_idx_k': nrm(ks[6], (DEPTH, n_pool, PAGE_SIZE, D_IDX)),
        'page_table': page_table,
        'norm_mix': 1.0 + nrm(ks[8], (DEPTH, D_MODEL), 0.02),
        'w_in': nrm(ks[9], (DEPTH, D_MODEL, N_IN), D_MODEL ** -0.5),
        'w_branch_sb': nrm(ks[10], (DEPTH, SB_W, D_MODEL), SB_W ** -0.5),
        'w_branch_dsa': nrm(ks[11], (DEPTH, DSA_W, D_MODEL), DSA_W ** -0.5),
        'w_out': nrm(ks[12], (DEPTH, D_MODEL, D_MODEL), D_MODEL ** -0.5),
        'norm_mlp': 1.0 + nrm(ks[13], (DEPTH, D_MODEL), 0.02),
        'w_up': nrm(ks[14], (DEPTH, D_MODEL, D_FF), D_MODEL ** -0.5),
        'w_down': nrm(ks[15], (DEPTH, D_FF, D_MODEL), D_FF ** -0.5),
        'norm_final': 1.0 + nrm(ks[16], (D_MODEL,), 0.02),
    }


def reference(x_prompt, x_sample, cache_sb_k, cache_sb_v, cache_dsa_k, cache_dsa_v, cache_idx_k, page_table,
              norm_mix, w_in, w_branch_sb, w_branch_dsa, w_out, norm_mlp, w_up, w_down, norm_final):
    pos_p = jnp.arange(x_prompt.shape[1], dtype=jnp.int32)
    pos_s = PAST_LEN + jnp.arange(x_sample.shape[1], dtype=jnp.int32)
    hp, hs = x_prompt, x_sample
    rows_p = [[] for _ in range(5)]
    rows_s = [[] for _ in range(5)]
    for l in range(DEPTH):
        past = tuple(gather_pages(c[l], page_table)
                     for c in (cache_sb_k, cache_sb_v, cache_dsa_k, cache_dsa_v, cache_idx_k))
        mix_p, new_p = token_mix(rms_norm(hp, norm_mix[l]), pos_p, w_in[l], w_branch_sb[l], w_branch_dsa[l], w_out[l])
        hp = hp + mix_p
        hp = hp + sq_relu_mlp(rms_norm(hp, norm_mlp[l]), w_up[l], w_down[l])
        mix_s, new_s = token_mix(rms_norm(hs, norm_mix[l]), pos_s, w_in[l], w_branch_sb[l], w_branch_dsa[l], w_out[l],
                                 past=past)
        hs = hs + mix_s
        hs = hs + sq_relu_mlp(rms_norm(hs, norm_mlp[l]), w_up[l], w_down[l])
        for lst, r in zip(rows_p, new_p):
            lst.append(r)
        for lst, r in zip(rows_s, new_s):
            lst.append(r)
    y_prompt = rms_norm(hp, norm_final)
    y_sample = rms_norm(hs, norm_final)
    sb_k_p, sb_v_p, ds_k_p, ds_v_p, ix_k_p = (jnp.stack(r, axis=0) for r in rows_p)
    sb_k_s, sb_v_s, ds_k_s, ds_v_s, ix_k_s = (jnp.stack(r, axis=0) for r in rows_s)
    return (y_prompt, y_sample, sb_k_p, sb_v_p, ds_k_p, ds_v_p, ix_k_p, sb_k_s, sb_v_s, ds_k_s, ds_v_s, ix_k_s)
```

```python
import functools

import jax
import jax.numpy as jnp
from jax import lax
from jax.experimental import pallas as pl
from jax.experimental.pallas import tpu as pltpu

HEAD_DIM = 128
D_IDX = 64
TOPK_MAX = 256
ROPE_THETA = 10000.0
EPS = 1e-6
LANES = 128
BF16_ROWS = 16
MASK_BIAS = -1e30
VMEM_LIMIT = 56 * 1024 * 1024

F32 = jnp.float32
BF16 = jnp.bfloat16
NT_DIMS = (((1,), (1,)), ((), ()))


def _params(*sem):
    return pltpu.CompilerParams(dimension_semantics=sem, vmem_limit_bytes=VMEM_LIMIT)


def _rmsnorm_kernel(x_ref, g_ref, o_ref):
    x = x_ref[...]
    ms = jnp.mean(x * x, axis=-1, keepdims=True)
    o_ref[...] = (x * lax.rsqrt(ms + EPS) * g_ref[...]).astype(o_ref.dtype)


def rmsnorm(x, g, out_dtype):
    m, d = x.shape
    tm = min(256, m)
    return pl.pallas_call(
        _rmsnorm_kernel,
        grid=(m // tm,),
        in_specs=[pl.BlockSpec((tm, d), lambda i: (i, 0)),
                  pl.BlockSpec((1, d), lambda i: (0, 0))],
        out_specs=pl.BlockSpec((tm, d), lambda i: (i, 0)),
        out_shape=jax.ShapeDtypeStruct((m, d), out_dtype),
        compiler_params=_params("parallel"),
        name="rmsnorm",
    )(x, g.reshape(1, d))


def _ep_store(acc, o_ref):
    o_ref[...] = acc.astype(o_ref.dtype)


def _ep_relu2(acc, o_ref):
    o_ref[...] = jnp.square(jnp.maximum(acc, 0.0)).astype(o_ref.dtype)


def _ep_sigmoid(acc, o_ref):
    o_ref[...] = jax.nn.sigmoid(acc).astype(o_ref.dtype)


def _ep_residual(acc, o_ref, res_ref):
    o_ref[...] = (res_ref[...] + acc).astype(o_ref.dtype)


def _ep_rope128(acc, o_ref, cos_ref, sin_ref):
    cos = cos_ref[...]
    sin = sin_ref[...]
    for c in range(acc.shape[1] // LANES):
        x = acc[:, c * LANES:(c + 1) * LANES]
        y = x * cos + pltpu.roll(x, LANES // 2, 1) * sin
        o_ref[:, c * LANES:(c + 1) * LANES] = y.astype(o_ref.dtype)


def _ep_rope64(acc, o_ref, cos_ref, sin_ref):
    cos = cos_ref[...]
    sin = sin_ref[...]
    lane = lax.broadcasted_iota(jnp.int32, cos.shape, 1)
    first_half = (lane % D_IDX) < (D_IDX // 2)
    for c in range(acc.shape[1] // LANES):
        x = acc[:, c * LANES:(c + 1) * LANES]
        partner = jnp.where(first_half, pltpu.roll(x, LANES - D_IDX // 2, 1),
                            pltpu.roll(x, D_IDX // 2, 1))
        y = x * cos + partner * sin
        o_ref[:, c * LANES:(c + 1) * LANES] = y.astype(o_ref.dtype)


def _mm_kernel(*refs, nk, n_extra, epilogue):
    x_ref, w_ref = refs[0], refs[1]
    extra = refs[2:2 + n_extra]
    o_ref = refs[2 + n_extra]
    if nk == 1:
        acc = jnp.dot(x_ref[...], w_ref[...], preferred_element_type=F32)
        epilogue(acc, o_ref, *extra)
        return
    acc_ref = refs[3 + n_extra]
    k = pl.program_id(2)

    @pl.when(k == 0)
    def _():
        acc_ref[...] = jnp.zeros_like(acc_ref)

    acc_ref[...] += jnp.dot(x_ref[...], w_ref[...], preferred_element_type=F32)

    @pl.when(k == nk - 1)
    def _():
        epilogue(acc_ref[...], o_ref, *extra)


def matmul(x, w, layer, col0, n, out_dtype, epilogue=_ep_store, extra=(), extra_specs=(),
           tm=1024, tn=512, tk=1024):
    m, kdim = x.shape
    tm = min(tm, m)
    tn = min(tn, n)
    tk = min(tk, kdim)
    assert m % tm == 0 and n % tn == 0 and kdim % tk == 0 and col0 % tn == 0
    nk = kdim // tk
    off = col0 // tn
    scratch = [pltpu.VMEM((tm, tn), F32)] if nk > 1 else []
    return pl.pallas_call(
        functools.partial(_mm_kernel, nk=nk, n_extra=len(extra), epilogue=epilogue),
        grid=(m // tm, n // tn, nk),
        in_specs=[pl.BlockSpec((tm, tk), lambda i, j, k: (i, k)),
                  pl.BlockSpec((None, tk, tn), lambda i, j, k: (layer, k, j + off)),
                  *extra_specs],
        out_specs=pl.BlockSpec((tm, tn), lambda i, j, k: (i, j)),
        out_shape=jax.ShapeDtypeStruct((m, n), out_dtype),
        scratch_shapes=scratch,
        compiler_params=_params("parallel", "parallel", "arbitrary"),
        name="matmul_" + epilogue.__name__[4:],
    )(x, w, *extra)


def _row_spec(tm, tn):
    return pl.BlockSpec((tm, tn), lambda i, j, k: (i, 0))


def _tile_spec(tm, tn):
    return pl.BlockSpec((tm, tn), lambda i, j, k: (i, j))


def _merge_kernel(a_ref, b_ref, wa_ref, wb_ref, ga_ref, gb_ref, o_ref):
    ya = jnp.dot(a_ref[...], wa_ref[...], preferred_element_type=F32)
    yb = jnp.dot(b_ref[...], wb_ref[...], preferred_element_type=F32)
    o_ref[...] = (ga_ref[...].astype(F32) * ya + gb_ref[...].astype(F32) * yb).astype(o_ref.dtype)


def merge_branches(o_sb, o_ds, w_sb, w_ds, gates, layer, d_model, tm=1024, tn=512):
    m, kdim = o_sb.shape
    tm = min(tm, m)
    nb = d_model // tn
    return pl.pallas_call(
        _merge_kernel,
        grid=(m // tm, nb),
        in_specs=[pl.BlockSpec((tm, kdim), lambda i, j: (i, 0)),
                  pl.BlockSpec((tm, kdim), lambda i, j: (i, 0)),
                  pl.BlockSpec((None, kdim, tn), lambda i, j: (layer, 0, j)),
                  pl.BlockSpec((None, kdim, tn), lambda i, j: (layer, 0, j)),
                  pl.BlockSpec((tm, tn), lambda i, j: (i, j)),
                  pl.BlockSpec((tm, tn), lambda i, j: (i, j + nb))],
        out_specs=pl.BlockSpec((tm, tn), lambda i, j: (i, j)),
        out_shape=jax.ShapeDtypeStruct((m, d_model), BF16),
        compiler_params=_params("parallel", "parallel"),
        name="merge_branches",
    )(o_sb, o_ds, w_sb, w_ds, gates, gates)


def _later_sum_matrix(n):
    j = lax.broadcasted_iota(jnp.int32, (n, n), 0)
    s = lax.broadcasted_iota(jnp.int32, (n, n), 1)
    return jnp.where(j > s, 1.0, 0.0).astype(BF16)


def _sum_later(x, u):
    hi = x.astype(BF16)
    r1 = x - hi.astype(F32)
    mid = r1.astype(BF16)
    lo = (r1 - mid.astype(F32)).astype(BF16)
    return (jnp.dot(hi, u, preferred_element_type=F32)
            + jnp.dot(mid, u, preferred_element_type=F32)
            + jnp.dot(lo, u, preferred_element_type=F32))


def _stick_break_block(z, allowed, after_block):
    softplus = jnp.log(1.0 + jnp.exp(-jnp.abs(z)))
    log_beta = jnp.minimum(z, 0.0) - softplus
    log_keep = log_beta - z
    if allowed is not None:
        log_keep = jnp.where(allowed, log_keep, 0.0)
    after = _sum_later(log_keep, _later_sum_matrix(z.shape[1])) + after_block
    w = jnp.exp(log_beta + after)
    if allowed is not None:
        w = jnp.where(allowed, w, 0.0)
    return w, jnp.sum(log_keep, axis=-1, keepdims=True)


def _sortable_key(score):
    score = jnp.where(score == 0.0, 0.0, score)
    bits = pltpu.bitcast(score, jnp.int32)
    return bits ^ ((bits >> 31) & jnp.int32(0x7FFFFFFF))


def _topk_select(key_ref, n_sel, n_cols):
    rows = key_ref.shape[0]
    want = jnp.float32(n_sel)

    def count_ge(cand):
        return jnp.sum(jnp.where(key_ref[...] >= cand, 1.0, 0.0), axis=-1, keepdims=True)

    int_min = jnp.int32(-2 ** 31)
    t0 = jnp.full((rows, 1), int_min, jnp.int32)
    cand0 = jnp.zeros((rows, 1), jnp.int32)
    t0 = jnp.where(count_ge(cand0) >= want, cand0, t0)

    def thr_body(i, t):
        cand = t + jnp.left_shift(jnp.int32(1), 30 - i)
        return jnp.where(count_ge(cand) >= want, cand, t)

    thr = lax.fori_loop(0, 31, thr_body, t0)

    key = key_ref[...]
    greater = key > thr
    tie = key == thr
    need = want - jnp.sum(jnp.where(greater, 1.0, 0.0), axis=-1, keepdims=True)
    col = lax.broadcasted_iota(jnp.int32, key.shape, 1)

    n_bits = max(1, (n_cols - 1).bit_length())

    def tie_body(i, jj):
        cand = jj + jnp.left_shift(jnp.int32(1), n_bits - 1 - i)
        cnt = jnp.sum(jnp.where((key_ref[...] == thr) & (col < cand), 1.0, 0.0),
                      axis=-1, keepdims=True)
        return jnp.where(cnt < need, cand, jj)

    jmax = lax.fori_loop(0, n_bits, tie_body, jnp.zeros((rows, 1), jnp.int32))
    return greater | (tie & (col <= jmax))


def _sb_prompt_kernel(q_ref, k_ref, v_ref, o_ref, *, blk, scale):
    qi = pl.program_id(2)
    q = q_ref[0]
    row = lax.broadcasted_iota(jnp.int32, (blk, blk), 0)
    col = lax.broadcasted_iota(jnp.int32, (blk, blk), 1)

    def body(i, carry):
        after_block, acc = carry
        kb = qi - i
        start = pl.multiple_of(kb * blk, blk)
        k = k_ref[0, pl.ds(start, blk), :].astype(BF16)
        v = v_ref[0, pl.ds(start, blk), :].astype(BF16)
        z = lax.dot_general(q, k, NT_DIMS, preferred_element_type=F32) * scale
        allowed = (kb * blk + col) < (qi * blk + row)
        w, keep_sum = _stick_break_block(z, allowed, after_block)
        acc = acc + jnp.dot(w.astype(BF16), v, preferred_element_type=F32)
        return after_block + keep_sum, acc

    init = (jnp.zeros((blk, 1), F32), jnp.zeros((blk, HEAD_DIM), F32))
    _, acc = lax.fori_loop(0, qi + 1, body, init)
    o_ref[0] = acc.astype(o_ref.dtype)


def sb_prompt(q, k, v, blk=128):
    b, t, w = q.shape
    h = w // HEAD_DIM
    blk = min(blk, t)
    return pl.pallas_call(
        functools.partial(_sb_prompt_kernel, blk=blk, scale=HEAD_DIM ** -0.5),
        grid=(b, h, t // blk),
        in_specs=[pl.BlockSpec((1, blk, HEAD_DIM), lambda bi, hi, qi: (bi, qi, hi)),
                  pl.BlockSpec((1, t, HEAD_DIM), lambda bi, hi, qi: (bi, 0, hi)),
                  pl.BlockSpec((1, t, HEAD_DIM), lambda bi, hi, qi: (bi, 0, hi))],
        out_specs=pl.BlockSpec((1, blk, HEAD_DIM), lambda bi, hi, qi: (bi, qi, hi)),
        out_shape=jax.ShapeDtypeStruct((b, t, w), BF16),
        compiler_params=_params("parallel", "parallel", "arbitrary"),
        name="sb_prompt",
    )(q, k, v)


def _dsa_prompt_kernel(qix_ref, kwq_ref, kwk_ref, q_ref, k_ref, v_ref, o_ref,
                       qstack_ref, wb_ref, key_ref, bias_ref,
                       *, bq, s_len, n_sel, n_pair, cchunk, idx_scale, scale):
    qi = pl.program_id(1)
    h = pl.program_id(2)

    @pl.when(h == 0)
    def _():
        kw = kwk_ref[0]
        lane = lax.broadcasted_iota(jnp.int32, kw.shape, 1)
        k_even = jnp.where(lane < D_IDX, kw, 0.0).astype(BF16)
        k_odd = jnp.where(lane >= D_IDX, pltpu.roll(kw, D_IDX, 1), 0.0).astype(BF16)
        for p in range(n_pair):
            qstack_ref[p * bq:(p + 1) * bq, :] = qix_ref[0, :, p * LANES:(p + 1) * LANES]
        wq = kwq_ref[0]
        for i in range(2 * n_pair):
            wb_ref[i * bq:(i + 1) * bq, :] = jnp.broadcast_to(
                wq[:, D_IDX + i:D_IDX + i + 1], (bq, LANES))
        qs = qstack_ref[...]
        rowq = lax.broadcasted_iota(jnp.int32, (bq, cchunk), 0) + qi * bq
        for c in range(s_len // cchunk):
            ke = k_even[c * cchunk:(c + 1) * cchunk, :]
            ko = k_odd[c * cchunk:(c + 1) * cchunk, :]
            r0 = lax.dot_general(qs, ke, NT_DIMS, preferred_element_type=F32)
            r1 = lax.dot_general(qs, ko, NT_DIMS, preferred_element_type=F32)
            sc = jnp.zeros((bq, cchunk), F32)
            for p in range(n_pair):
                w0 = wb_ref[(2 * p) * bq:(2 * p + 1) * bq, :]
                w1 = wb_ref[(2 * p + 1) * bq:(2 * p + 2) * bq, :]
                if cchunk > LANES:
                    w0 = jnp.concatenate([w0] * (cchunk // LANES), axis=1)
                    w1 = jnp.concatenate([w1] * (cchunk // LANES), axis=1)
                sc = sc + w0 * jnp.maximum(r0[p * bq:(p + 1) * bq, :], 0.0)
                sc = sc + w1 * jnp.maximum(r1[p * bq:(p + 1) * bq, :], 0.0)
            colk = lax.broadcasted_iota(jnp.int32, (bq, cchunk), 1) + c * cchunk
            sc = jnp.where(colk <= rowq, sc * idx_scale, -jnp.inf)
            key_ref[:, c * cchunk:(c + 1) * cchunk] = _sortable_key(sc)
        sel = _topk_select(key_ref, n_sel, s_len)
        rowf = lax.broadcasted_iota(jnp.int32, (bq, s_len), 0) + qi * bq
        colf = lax.broadcasted_iota(jnp.int32, (bq, s_len), 1)
        bias_ref[...] = jnp.where(sel & (colf <= rowf), 0.0, MASK_BIAS)

    q = q_ref[0]
    k = k_ref[0].astype(BF16)
    v = v_ref[0].astype(BF16)
    logits = lax.dot_general(q, k, NT_DIMS, preferred_element_type=F32) * scale + bias_ref[...]
    m = jnp.max(logits, axis=-1, keepdims=True)
    p = jnp.exp(logits - m)
    denom = jnp.sum(p, axis=-1, keepdims=True)
    out = jnp.dot(p.astype(BF16), v, preferred_element_type=F32) / denom
    o_ref[0] = out.astype(o_ref.dtype)


def dsa_prompt(q_ix, kw, q, k, v, n_idx, bq=128):
    b, t, w = q.shape
    h = w // HEAD_DIM
    bq = min(bq, t)
    n_sel = min(TOPK_MAX, t // 4)
    n_pair = n_idx // 2
    cchunk = min(256, t)
    kern = functools.partial(
        _dsa_prompt_kernel, bq=bq, s_len=t, n_sel=n_sel, n_pair=n_pair, cchunk=cchunk,
        idx_scale=(D_IDX ** -0.5) * (n_idx ** -0.5), scale=HEAD_DIM ** -0.5)
    return pl.pallas_call(
        kern,
        grid=(b, t // bq, h),
        in_specs=[pl.BlockSpec((1, bq, n_idx * D_IDX), lambda bi, qi, hi: (bi, qi, 0)),
                  pl.BlockSpec((1, bq, LANES), lambda bi, qi, hi: (bi, qi, 0)),
                  pl.BlockSpec((1, t, LANES), lambda bi, qi, hi: (bi, 0, 0)),
                  pl.BlockSpec((1, bq, HEAD_DIM), lambda bi, qi, hi: (bi, qi, hi)),
                  pl.BlockSpec((1, t, HEAD_DIM), lambda bi, qi, hi: (bi, 0, hi)),
                  pl.BlockSpec((1, t, HEAD_DIM), lambda bi, qi, hi: (bi, 0, hi))],
        out_specs=pl.BlockSpec((1, bq, HEAD_DIM), lambda bi, qi, hi: (bi, qi, hi)),
        out_shape=jax.ShapeDtypeStruct((b, t, w), BF16),
        scratch_shapes=[pltpu.VMEM((n_pair * bq, LANES), BF16),
                        pltpu.VMEM((2 * n_pair * bq, LANES), F32),
                        pltpu.VMEM((bq, t), jnp.int32),
                        pltpu.VMEM((bq, t), F32)],
        compiler_params=_params("parallel", "parallel", "arbitrary"),
        name="dsa_prompt",
    )(q_ix, kw, kw, q, k, v)


def _head_rows(ref, hh, n_heads, page):
    return ref[0, pl.ds(hh, page, stride=n_heads), :].astype(BF16)


def _head_logits(q_ref, k_ref, n_t, n_heads, page):
    zs = []
    for hh in range(n_heads):
        z = lax.dot_general(q_ref[0, hh], _head_rows(k_ref, hh, n_heads, page), NT_DIMS,
                            preferred_element_type=F32)
        zs.append(z[:n_t])
    return jnp.concatenate(zs, axis=0)


def _head_weighted_values(w, v_ref, n_t, n_heads, page, q_rows):
    outs = []
    pad = jnp.zeros((q_rows - n_t, page), F32)
    for hh in range(n_heads):
        w_h = jnp.concatenate([w[hh * n_t:(hh + 1) * n_t], pad], axis=0).astype(BF16)
        o = jnp.dot(w_h, _head_rows(v_ref, hh, n_heads, page), preferred_element_type=F32)
        outs.append(o[:n_t])
    return outs


def _sb_sample_kernel(pt_ref, q_ref, kn_ref, vn_ref, kp_ref, vp_ref, o_ref, acc_ref, after_ref,
                      *, n_t, n_heads, page, scale):
    p = pl.program_id(1)
    q_rows = q_ref.shape[2]

    def step(k_ref, v_ref, is_new):
        z = _head_logits(q_ref, k_ref, n_t, n_heads, page) * scale
        allowed = None
        if is_new:
            row = lax.broadcasted_iota(jnp.int32, z.shape, 0)
            col = lax.broadcasted_iota(jnp.int32, z.shape, 1)
            allowed = col < (row % n_t)
        w, keep_sum = _stick_break_block(z, allowed, after_ref[...])
        for hh, o in enumerate(_head_weighted_values(w, v_ref, n_t, n_heads, page, q_rows)):
            acc_ref[hh] += o
        after_ref[...] += keep_sum

    @pl.when(p == 0)
    def _():
        acc_ref[...] = jnp.zeros_like(acc_ref)
        after_ref[...] = jnp.zeros_like(after_ref)
        step(kn_ref, vn_ref, True)

    @pl.when(p > 0)
    def _():
        step(kp_ref, vp_ref, False)

    @pl.when(p == pl.num_programs(1) - 1)
    def _():
        for hh in range(n_heads):
            o_ref[0, :, hh * HEAD_DIM:(hh + 1) * HEAD_DIM] = acc_ref[hh]


def sb_sample(page_table, q_heads, k_new, v_new, cache_k, cache_v, layer, n_t):
    db, n_heads, q_rows, _ = q_heads.shape
    n_pages = page_table.shape[1]
    rows = cache_k.shape[2]
    page = rows // n_heads
    w = n_heads * HEAD_DIM

    def page_map(bi, pi, pt):
        return (layer, pt[bi, n_pages - jnp.maximum(pi, 1)], 0, 0)

    grid_spec = pltpu.PrefetchScalarGridSpec(
        num_scalar_prefetch=1,
        grid=(db, n_pages + 1),
        in_specs=[pl.BlockSpec((1, n_heads, q_rows, HEAD_DIM), lambda bi, pi, pt: (bi, 0, 0, 0)),
                  pl.BlockSpec((1, rows, HEAD_DIM), lambda bi, pi, pt: (bi, 0, 0)),
                  pl.BlockSpec((1, rows, HEAD_DIM), lambda bi, pi, pt: (bi, 0, 0)),
                  pl.BlockSpec((None, 1, rows, HEAD_DIM), page_map),
                  pl.BlockSpec((None, 1, rows, HEAD_DIM), page_map)],
        out_specs=pl.BlockSpec((1, n_t, w), lambda bi, pi, pt: (bi, 0, 0)),
        scratch_shapes=[pltpu.VMEM((n_heads, n_t, HEAD_DIM), F32),
                        pltpu.VMEM((n_heads * n_t, 1), F32)],
    )
    return pl.pallas_call(
        functools.partial(_sb_sample_kernel, n_t=n_t, n_heads=n_heads, page=page,
                          scale=HEAD_DIM ** -0.5),
        grid_spec=grid_spec,
        out_shape=jax.ShapeDtypeStruct((db, n_t, w), F32),
        compiler_params=_params("parallel", "arbitrary"),
        name="sb_sample",
    )(page_table, q_heads, k_new, v_new, cache_k, cache_v)


def _idx_sample_kernel(pt_ref, q_ref, wb_ref, kn_ref, kp_ref, o_ref, *, n_t, n_idx, idx_scale):
    p = pl.program_id(1)
    last = pl.num_programs(1) - 1

    def score(k32):
        rel = lax.dot_general(q_ref[0], k32.astype(BF16), NT_DIMS, preferred_element_type=F32)
        rel = jnp.maximum(rel, 0.0) * wb_ref[0]
        return jnp.sum(rel.reshape(n_idx, n_t, rel.shape[1]), axis=0) * idx_scale

    @pl.when(p < last)
    def _():
        o_ref[0] = score(kp_ref[0])

    @pl.when(p == last)
    def _():
        sc = score(kn_ref[0])
        row = lax.broadcasted_iota(jnp.int32, sc.shape, 0)
        col = lax.broadcasted_iota(jnp.int32, sc.shape, 1)
        o_ref[0] = jnp.where(col <= row, sc, -jnp.inf)


def idx_sample(page_table, q_rows, w_rows, k_new, cache_ik, layer, n_t, n_idx):
    db, rows, _ = q_rows.shape
    n_pages = page_table.shape[1]
    page = cache_ik.shape[2]

    def page_map(bi, pi, pt):
        return (layer, pt[bi, jnp.minimum(pi, n_pages - 1)], 0, 0)

    grid_spec = pltpu.PrefetchScalarGridSpec(
        num_scalar_prefetch=1,
        grid=(db, n_pages + 1),
        in_specs=[pl.BlockSpec((1, rows, D_IDX), lambda bi, pi, pt: (bi, 0, 0)),
                  pl.BlockSpec((1, rows, page), lambda bi, pi, pt: (bi, 0, 0)),
                  pl.BlockSpec((1, page, D_IDX), lambda bi, pi, pt: (bi, 0, 0)),
                  pl.BlockSpec((None, 1, page, D_IDX), page_map)],
        out_specs=pl.BlockSpec((1, n_t, page), lambda bi, pi, pt: (bi, 0, pi)),
    )
    return pl.pallas_call(
        functools.partial(_idx_sample_kernel, n_t=n_t, n_idx=n_idx,
                          idx_scale=(D_IDX ** -0.5) * (n_idx ** -0.5)),
        grid_spec=grid_spec,
        out_shape=jax.ShapeDtypeStruct((db, n_t, (n_pages + 1) * page), F32),
        compiler_params=_params("parallel", "arbitrary"),
        name="idx_sample",
    )(page_table, q_rows, w_rows, k_new, cache_ik)


def _select_sample_kernel(s_ref, o_ref, key_ref, *, n_sel, n_cols):
    sc = s_ref[0]
    key_ref[...] = _sortable_key(sc)
    sel = _topk_select(key_ref, n_sel, n_cols)
    o_ref[0] = jnp.where(sel & (sc > -jnp.inf), 0.0, MASK_BIAS)


def select_sample(scores, n_sel):
    db, n_t, c = scores.shape
    return pl.pallas_call(
        functools.partial(_select_sample_kernel, n_sel=n_sel, n_cols=c),
        grid=(db,),
        in_specs=[pl.BlockSpec((1, n_t, c), lambda bi: (bi, 0, 0))],
        out_specs=pl.BlockSpec((1, n_t, c), lambda bi: (bi, 0, 0)),
        out_shape=jax.ShapeDtypeStruct((db, n_t, c), F32),
        scratch_shapes=[pltpu.VMEM((n_t, c), jnp.int32)],
        compiler_params=_params("parallel"),
        name="select_sample",
    )(scores)


def _dsa_sample_kernel(pt_ref, q_ref, bias_ref, kn_ref, vn_ref, kp_ref, vp_ref, o_ref,
                       acc_ref, m_ref, l_ref, *, n_t, n_heads, page, scale):
    p = pl.program_id(1)
    q_rows = q_ref.shape[2]

    def step(k_ref, v_ref):
        bias = jnp.concatenate([bias_ref[0]] * n_heads, axis=0)
        logits = _head_logits(q_ref, k_ref, n_t, n_heads, page) * scale + bias
        m_old = m_ref[...]
        m_new = jnp.maximum(m_old, jnp.max(logits, axis=-1, keepdims=True))
        alpha = jnp.exp(m_old - m_new)
        pr = jnp.where(bias == 0.0, jnp.exp(logits - m_new), 0.0)
        l_ref[...] = alpha * l_ref[...] + jnp.sum(pr, axis=-1, keepdims=True)
        for hh, o in enumerate(_head_weighted_values(pr, v_ref, n_t, n_heads, page, q_rows)):
            acc_ref[hh] = alpha[hh * n_t:(hh + 1) * n_t] * acc_ref[hh] + o
        m_ref[...] = m_new

    @pl.when(p == 0)
    def _():
        acc_ref[...] = jnp.zeros_like(acc_ref)
        l_ref[...] = jnp.zeros_like(l_ref)
        m_ref[...] = jnp.full_like(m_ref, MASK_BIAS)
        step(kn_ref, vn_ref)

    @pl.when(p > 0)
    def _():
        step(kp_ref, vp_ref)

    @pl.when(p == pl.num_programs(1) - 1)
    def _():
        inv = 1.0 / l_ref[...]
        for hh in range(n_heads):
            o_ref[0, :, hh * HEAD_DIM:(hh + 1) * HEAD_DIM] = (
                acc_ref[hh] * inv[hh * n_t:(hh + 1) * n_t])


def dsa_sample(page_table, q_heads, bias, k_new, v_new, cache_k, cache_v, layer, n_t):
    db, n_heads, q_rows, _ = q_heads.shape
    n_pages = page_table.shape[1]
    rows = cache_k.shape[2]
    page = rows // n_heads
    w = n_heads * HEAD_DIM

    def page_map(bi, pi, pt):
        return (layer, pt[bi, jnp.maximum(pi, 1) - 1], 0, 0)

    def bias_map(bi, pi, pt):
        return (bi, 0, jnp.where(pi == 0, n_pages, pi - 1))

    grid_spec = pltpu.PrefetchScalarGridSpec(
        num_scalar_prefetch=1,
        grid=(db, n_pages + 1),
        in_specs=[pl.BlockSpec((1, n_heads, q_rows, HEAD_DIM), lambda bi, pi, pt: (bi, 0, 0, 0)),
                  pl.BlockSpec((1, n_t, page), bias_map),
                  pl.BlockSpec((1, rows, HEAD_DIM), lambda bi, pi, pt: (bi, 0, 0)),
                  pl.BlockSpec((1, rows, HEAD_DIM), lambda bi, pi, pt: (bi, 0, 0)),
                  pl.BlockSpec((None, 1, rows, HEAD_DIM), page_map),
                  pl.BlockSpec((None, 1, rows, HEAD_DIM), page_map)],
        out_specs=pl.BlockSpec((1, n_t, w), lambda bi, pi, pt: (bi, 0, 0)),
        scratch_shapes=[pltpu.VMEM((n_heads, n_t, HEAD_DIM), F32),
                        pltpu.VMEM((n_heads * n_t, 1), F32),
                        pltpu.VMEM((n_heads * n_t, 1), F32)],
    )
    return pl.pallas_call(
        functools.partial(_dsa_sample_kernel, n_t=n_t, n_heads=n_heads, page=page,
                          scale=HEAD_DIM ** -0.5),
        grid_spec=grid_spec,
        out_shape=jax.ShapeDtypeStruct((db, n_t, w), F32),
        compiler_params=_params("parallel", "arbitrary"),
        name="dsa_sample",
    )(page_table, q_heads, bias, k_new, v_new, cache_k, cache_v)


def _rope_tables(pos, d, reps_rows):
    inv = ROPE_THETA ** (-jnp.arange(0, d, 2, dtype=F32) / d)
    ang = pos.astype(F32)[:, None] * inv[None, :]
    cos = jnp.cos(ang)
    sin = jnp.sin(ang)
    cosf = jnp.tile(jnp.concatenate([cos, cos], axis=-1), (reps_rows, LANES // d))
    sinf = jnp.tile(jnp.concatenate([-sin, sin], axis=-1), (reps_rows, LANES // d))
    return cosf, sinf


def _head_major_queries(q, n_heads, q_rows):
    db, n_t, _ = q.shape
    q4 = q.reshape(db, n_t, n_heads, HEAD_DIM).transpose(0, 2, 1, 3)
    return jnp.pad(q4, ((0, 0), (0, 0), (0, q_rows - n_t), (0, 0)))


def _new_rows_as_page(x, n_heads, page):
    db, n_t, _ = x.shape
    x = jnp.pad(x, ((0, 0), (0, page - n_t), (0, 0)))
    return x.reshape(db, page * n_heads, HEAD_DIM)


def _pad_rows(x, rows):
    return jnp.pad(x, ((0, 0), (0, rows - x.shape[1]), (0, 0)))


def kernel(x_prompt, x_sample, cache_sb_k, cache_sb_v, cache_dsa_k, cache_dsa_v, cache_idx_k,
           page_table, norm_mix, w_in, w_branch_sb, w_branch_dsa, w_out, norm_mlp, w_up, w_down,
           norm_final):
    bsz, seq, d_model = x_prompt.shape
    db, n_t, _ = x_sample.shape
    depth, n_pool, page, h_sb, _ = cache_sb_k.shape
    h_dsa = cache_dsa_k.shape[3]
    n_pages = page_table.shape[1]
    past_len = n_pages * page
    sb_w = h_sb * HEAD_DIM
    dsa_w = h_dsa * HEAD_DIM
    n_in = w_in.shape[2]
    n_idx = (n_in - 3 * sb_w - 3 * dsa_w - D_IDX - 2 * d_model) // (D_IDX + 1)
    idx_w = n_idx * D_IDX
    c_qsb, c_ksb, c_vsb = 0, sb_w, 2 * sb_w
    c_qds, c_kds, c_vds = 3 * sb_w, 3 * sb_w + dsa_w, 3 * sb_w + 2 * dsa_w
    c_qix = 3 * sb_w + 3 * dsa_w
    c_kw = c_qix + idx_w
    c_gate = c_kw + D_IDX + n_idx
    assert D_IDX + n_idx <= LANES and n_t <= page

    w_main = w_in[:, :, :c_kw].astype(BF16)
    w_kw = jnp.pad(w_in[:, :, c_kw:c_gate], ((0, 0), (0, 0), (0, LANES - (c_gate - c_kw)))).astype(BF16)
    w_gate = w_in[:, :, c_gate:].astype(BF16)
    w_bsb = w_branch_sb.astype(BF16)
    w_bds = w_branch_dsa.astype(BF16)
    w_o = w_out.astype(BF16)
    w_u = w_up.astype(BF16)
    w_d = w_down.astype(BF16)

    pos_p = jnp.arange(seq, dtype=jnp.int32)
    pos_s = past_len + jnp.arange(n_t, dtype=jnp.int32)
    tables = {}
    for name, pos, reps in (("p", pos_p, bsz), ("s", pos_s, db)):
        cos128, sin128 = _rope_tables(pos, HEAD_DIM, reps)
        cos64, sin64 = _rope_tables(pos, D_IDX, reps)
        lane = jnp.arange(LANES)[None, :]
        cos_kw = jnp.where(lane < D_IDX, cos64, 1.0)
        sin_kw = jnp.where(lane < D_IDX, sin64, 0.0)
        tables[name] = (cos128, sin128, cos64, sin64, cos_kw, sin_kw)

    cache_sb_k2 = cache_sb_k.reshape(depth, n_pool, page * h_sb, HEAD_DIM)
    cache_sb_v2 = cache_sb_v.reshape(depth, n_pool, page * h_sb, HEAD_DIM)
    cache_ds_k2 = cache_dsa_k.reshape(depth, n_pool, page * h_dsa, HEAD_DIM)
    cache_ds_v2 = cache_dsa_v.reshape(depth, n_pool, page * h_dsa, HEAD_DIM)
    q_rows_pad = -(-n_t // BF16_ROWS) * BF16_ROWS

    def project(xn, layer, tbl):
        cos128, sin128, cos64, sin64, cos_kw, sin_kw = tbl
        m = xn.shape[0]
        tm = min(1024, m)
        rope_specs = (_row_spec(tm, LANES), _row_spec(tm, LANES))
        mm = functools.partial(matmul, xn, tm=tm, tk=d_model)
        q_sb = mm(w_main, layer, c_qsb, sb_w, BF16)
        k_sb = mm(w_main, layer, c_ksb, sb_w, F32)
        v_sb = mm(w_main, layer, c_vsb, sb_w, F32)
        q_ds = mm(w_main, layer, c_qds, dsa_w, BF16, _ep_rope128, (cos128, sin128), rope_specs)
        k_ds = mm(w_main, layer, c_kds, dsa_w, F32, _ep_rope128, (cos128, sin128), rope_specs)
        v_ds = mm(w_main, layer, c_vds, dsa_w, F32)
        q_ix = mm(w_main, layer, c_qix, idx_w, BF16, _ep_rope64, (cos64, sin64), rope_specs)
        kw = mm(w_kw, layer, 0, LANES, F32, _ep_rope64, (cos_kw, sin_kw), rope_specs)
        gates = mm(w_gate, layer, 0, 2 * d_model, BF16, _ep_sigmoid)
        return q_sb, k_sb, v_sb, q_ds, k_ds, v_ds, q_ix, kw, gates

    def finish_layer(h, o_sb, o_ds, gates, layer):
        m = h.shape[0]
        tm = min(1024, m)
        merged = merge_branches(o_sb, o_ds, w_bsb, w_bds, gates, layer, d_model, tm=tm)
        h = matmul(merged, w_o, layer, 0, d_model, F32, _ep_residual, (h,), (_tile_spec(tm, 512),),
                   tm=tm, tk=d_model)
        xn = rmsnorm(h, norm_mlp[layer], BF16)
        hid = matmul(xn, w_u, layer, 0, w_up.shape[2], BF16, _ep_relu2, tm=tm, tk=d_model)
        h = matmul(hid, w_d, layer, 0, d_model, F32, _ep_residual, (h,), (_tile_spec(tm, 512),),
                   tm=tm, tk=2048)
        return h

    hp = x_prompt.reshape(bsz * seq, d_model)
    hs = x_sample.reshape(db * n_t, d_model)
    rows_p = [[] for _ in range(5)]
    rows_s = [[] for _ in range(5)]
    n_sel_s = min(TOPK_MAX, (past_len + n_t) // 4)

    for layer in range(depth):
        xn = rmsnorm(hp, norm_mix[layer], BF16)
        q_sb, k_sb, v_sb, q_ds, k_ds, v_ds, q_ix, kw, gates = project(xn, layer, tables["p"])
        r3 = lambda a: a.reshape(bsz, seq, a.shape[1])
        o_sb = sb_prompt(r3(q_sb), r3(k_sb), r3(v_sb))
        o_ds = dsa_prompt(r3(q_ix), r3(kw), r3(q_ds), r3(k_ds), r3(v_ds), n_idx)
        hp = finish_layer(hp, o_sb.reshape(bsz * seq, sb_w), o_ds.reshape(bsz * seq, dsa_w),
                          gates, layer)
        for lst, r in zip(rows_p, (k_sb, v_sb, k_ds, v_ds, kw[:, :D_IDX])):
            lst.append(r)

        xn = rmsnorm(hs, norm_mix[layer], BF16)
        q_sb, k_sb, v_sb, q_ds, k_ds, v_ds, q_ix, kw, gates = project(xn, layer, tables["s"])
        s3 = lambda a: a.reshape(db, n_t, a.shape[1])
        o_sb = sb_sample(page_table, _head_major_queries(s3(q_sb), h_sb, q_rows_pad),
                         _new_rows_as_page(s3(k_sb), h_sb, page),
                         _new_rows_as_page(s3(v_sb), h_sb, page),
                         cache_sb_k2, cache_sb_v2, layer, n_t)
        kw3 = s3(kw)
        q_rows = s3(q_ix).reshape(db, n_t, n_idx, D_IDX).transpose(0, 2, 1, 3).reshape(
            db, n_idx * n_t, D_IDX)
        w_rows = jnp.broadcast_to(
            kw3[:, :, D_IDX:D_IDX + n_idx].transpose(0, 2, 1).reshape(db, n_idx * n_t, 1),
            (db, n_idx * n_t, page))
        scores = idx_sample(page_table, q_rows, w_rows, _pad_rows(kw3[:, :, :D_IDX], page),
                            cache_idx_k, layer, n_t, n_idx)
        bias = select_sample(scores, n_sel_s)
        o_ds = dsa_sample(page_table, _head_major_queries(s3(q_ds), h_dsa, q_rows_pad), bias,
                          _new_rows_as_page(s3(k_ds), h_dsa, page),
                          _new_rows_as_page(s3(v_ds), h_dsa, page),
                          cache_ds_k2, cache_ds_v2, layer, n_t)
        hs = finish_layer(hs, o_sb.reshape(db * n_t, sb_w).astype(BF16),
                          o_ds.reshape(db * n_t, dsa_w).astype(BF16), gates, layer)
        for lst, r in zip(rows_s, (k_sb, v_sb, k_ds, v_ds, kw[:, :D_IDX])):
            lst.append(r)

    y_prompt = rmsnorm(hp, norm_final, F32).reshape(bsz, seq, d_model)
    y_sample = rmsnorm(hs, norm_final, F32).reshape(db, n_t, d_model)

    def stack(lst, lead, heads):
        a = jnp.stack(lst, axis=0)
        if heads is None:
            return a.reshape((depth,) + lead + (D_IDX,))
        return a.reshape((depth,) + lead + (heads, HEAD_DIM))

    outs_p = [stack(rows_p[0], (bsz, seq), h_sb), stack(rows_p[1], (bsz, seq), h_sb),
              stack(rows_p[2], (bsz, seq), h_dsa), stack(rows_p[3], (bsz, seq), h_dsa),
              stack(rows_p[4], (bsz, seq), None)]
    outs_s = [stack(rows_s[0], (db, n_t), h_sb), stack(rows_s[1], (db, n_t), h_sb),
              stack(rows_s[2], (db, n_t), h_dsa), stack(rows_s[3], (db, n_t), h_dsa),
              stack(rows_s[4], (db, n_t), None)]
    return (y_prompt, y_sample, *outs_p, *outs_s)
```

```python
import functools

import jax
import jax.numpy as jnp
from jax import lax
from jax.experimental import pallas as pl
from jax.experimental.pallas import tpu as pltpu

HEAD_DIM = 128
D_IDX = 64
TOPK_MAX = 256
ROPE_THETA = 10000.0
EPS = 1e-6
LANES = 128
BF16_ROWS = 16
MASK_BIAS = -1e30
EXP_UNDERFLOW = -104.0
VMEM_LIMIT = 60000 * 1024
PROJ_TM = 2048
PROJ_TN = 256
DOWN_TN = 1024

F32 = jnp.float32
BF16 = jnp.bfloat16
NT_DIMS = (((1,), (1,)), ((), ()))


def _params(*sem):
    return pltpu.CompilerParams(dimension_semantics=sem, vmem_limit_bytes=VMEM_LIMIT)


def _rmsnorm_kernel(x_ref, g_ref, o_ref):
    x = x_ref[...]
    ms = jnp.mean(x * x, axis=-1, keepdims=True)
    o_ref[...] = (x * lax.rsqrt(ms + EPS) * g_ref[...]).astype(o_ref.dtype)


def rmsnorm(x, g, out_dtype):
    m, d = x.shape
    tm = min(256, m)
    return pl.pallas_call(
        _rmsnorm_kernel,
        grid=(m // tm,),
        in_specs=[pl.BlockSpec((tm, d), lambda i: (i, 0)),
                  pl.BlockSpec((1, d), lambda i: (0, 0))],
        out_specs=pl.BlockSpec((tm, d), lambda i: (i, 0)),
        out_shape=jax.ShapeDtypeStruct((m, d), out_dtype),
        compiler_params=_params("parallel"),
        name="rmsnorm",
    )(x, g.reshape(1, d))


def _ep_store(acc, o_ref):
    o_ref[...] = acc.astype(o_ref.dtype)


def _ep_relu2(acc, o_ref):
    o_ref[...] = jnp.square(jnp.maximum(acc, 0.0)).astype(o_ref.dtype)


def _ep_sigmoid(acc, o_ref):
    o_ref[...] = jax.nn.sigmoid(acc).astype(o_ref.dtype)


def _ep_residual(acc, o_ref, res_ref):
    o_ref[...] = (res_ref[...] + acc).astype(o_ref.dtype)


def _ep_rope128(acc, o_ref, cos_ref, sin_ref):
    cos = cos_ref[...]
    sin = sin_ref[...]
    for c in range(acc.shape[1] // LANES):
        x = acc[:, c * LANES:(c + 1) * LANES]
        y = x * cos + pltpu.roll(x, LANES // 2, 1) * sin
        o_ref[:, c * LANES:(c + 1) * LANES] = y.astype(o_ref.dtype)


def _ep_rope64(acc, o_ref, cos_ref, sin_ref):
    cos = cos_ref[...]
    sin = sin_ref[...]
    lane = lax.broadcasted_iota(jnp.int32, cos.shape, 1)
    first_half = (lane % D_IDX) < (D_IDX // 2)
    for c in range(acc.shape[1] // LANES):
        x = acc[:, c * LANES:(c + 1) * LANES]
        partner = jnp.where(first_half, pltpu.roll(x, LANES - D_IDX // 2, 1),
                            pltpu.roll(x, D_IDX // 2, 1))
        y = x * cos + partner * sin
        o_ref[:, c * LANES:(c + 1) * LANES] = y.astype(o_ref.dtype)


def _mm_kernel(*refs, nk, n_extra, epilogue, copy_dtype):
    x_ref, w_ref = refs[0], refs[1]
    extra = refs[2:2 + n_extra]
    o_ref = refs[2 + n_extra]
    n_out = 1 if copy_dtype is None else 2

    def finish(acc):
        epilogue(acc, o_ref, *extra)
        if copy_dtype is not None:
            refs[3 + n_extra][...] = o_ref[...].astype(copy_dtype)

    if nk == 1:
        finish(jnp.dot(x_ref[...], w_ref[...], preferred_element_type=F32))
        return
    acc_ref = refs[2 + n_extra + n_out]
    k = pl.program_id(2)

    @pl.when(k == 0)
    def _():
        acc_ref[...] = jnp.zeros_like(acc_ref)

    acc_ref[...] += jnp.dot(x_ref[...], w_ref[...], preferred_element_type=F32)

    @pl.when(k == nk - 1)
    def _():
        finish(acc_ref[...])


def column_blocks(w, tn):
    d, k, n = w.shape
    return w.reshape(d, k, n // tn, tn).transpose(0, 2, 1, 3)


def matmul(x, w, layer, col0, n, out_dtype, epilogue=_ep_store, extra=(), extra_specs=(),
           tm=1024, tn=512, tk=1024, copy_dtype=None):
    m, kdim = x.shape
    blocked = w.ndim == 4
    tm = min(tm, m)
    tn = w.shape[3] if blocked else min(tn, n)
    tk = min(tk, kdim)
    assert m % tm == 0 and n % tn == 0 and kdim % tk == 0 and col0 % tn == 0
    nk = kdim // tk
    off = col0 // tn
    if blocked:
        w_spec = pl.BlockSpec((None, None, tk, tn), lambda i, j, k: (layer, j + off, k, 0))
    else:
        w_spec = pl.BlockSpec((None, tk, tn), lambda i, j, k: (layer, k, j + off))
    scratch = [pltpu.VMEM((tm, tn), F32)] if nk > 1 else []
    out_spec = pl.BlockSpec((tm, tn), lambda i, j, k: (i, j))
    out_shape = jax.ShapeDtypeStruct((m, n), out_dtype)
    if copy_dtype is not None:
        out_spec = [out_spec, out_spec]
        out_shape = (out_shape, jax.ShapeDtypeStruct((m, n), copy_dtype))
    return pl.pallas_call(
        functools.partial(_mm_kernel, nk=nk, n_extra=len(extra), epilogue=epilogue,
                          copy_dtype=copy_dtype),
        grid=(m // tm, n // tn, nk),
        in_specs=[pl.BlockSpec((tm, tk), lambda i, j, k: (i, k)), w_spec, *extra_specs],
        out_specs=out_spec,
        out_shape=out_shape,
        scratch_shapes=scratch,
        compiler_params=_params("parallel", "parallel", "arbitrary"),
        name="matmul_" + epilogue.__name__[4:],
    )(x, w, *extra)


def _row_spec(tm, tn):
    return pl.BlockSpec((tm, tn), lambda i, j, k: (i, 0))


def _tile_spec(tm, tn):
    return pl.BlockSpec((tm, tn), lambda i, j, k: (i, j))


def _merge_kernel(a_ref, b_ref, wa_ref, wb_ref, ga_ref, gb_ref, o_ref):
    ya = jnp.dot(a_ref[...], wa_ref[...], preferred_element_type=F32)
    yb = jnp.dot(b_ref[...], wb_ref[...], preferred_element_type=F32)
    o_ref[...] = (ga_ref[...].astype(F32) * ya + gb_ref[...].astype(F32) * yb).astype(o_ref.dtype)


def merge_branches(o_sb, o_ds, w_sb, w_ds, gates, layer, d_model, tm=1024, tn=512):
    m, kdim = o_sb.shape
    tm = min(tm, m)
    nb = d_model // tn
    return pl.pallas_call(
        _merge_kernel,
        grid=(m // tm, nb),
        in_specs=[pl.BlockSpec((tm, kdim), lambda i, j: (i, 0)),
                  pl.BlockSpec((tm, kdim), lambda i, j: (i, 0)),
                  pl.BlockSpec((None, kdim, tn), lambda i, j: (layer, 0, j)),
                  pl.BlockSpec((None, kdim, tn), lambda i, j: (layer, 0, j)),
                  pl.BlockSpec((tm, tn), lambda i, j: (i, j)),
                  pl.BlockSpec((tm, tn), lambda i, j: (i, j + nb))],
        out_specs=pl.BlockSpec((tm, tn), lambda i, j: (i, j)),
        out_shape=jax.ShapeDtypeStruct((m, d_model), BF16),
        compiler_params=_params("parallel", "parallel"),
        name="merge_branches",
    )(o_sb, o_ds, w_sb, w_ds, gates, gates)


def _later_sum_matrix(n):
    j = lax.broadcasted_iota(jnp.int32, (n, n), 0)
    s = lax.broadcasted_iota(jnp.int32, (n, n), 1)
    return jnp.where(j > s, 1.0, 0.0).astype(BF16)


def _sum_later(x, u):
    hi = x.astype(BF16)
    lo = (x - hi.astype(F32)).astype(BF16)
    return (jnp.dot(hi, u, preferred_element_type=F32)
            + jnp.dot(lo, u, preferred_element_type=F32))


def _stick_break_block(z, allowed, after_block):
    softplus = jnp.log(1.0 + jnp.exp(-jnp.abs(z)))
    log_beta = jnp.minimum(z, 0.0) - softplus
    log_keep = log_beta - z
    if allowed is not None:
        log_keep = jnp.where(allowed, log_keep, 0.0)
    after = _sum_later(log_keep, _later_sum_matrix(z.shape[1])) + after_block
    w = jnp.exp(log_beta + after)
    if allowed is not None:
        w = jnp.where(allowed, w, 0.0)
    return w, jnp.sum(log_keep, axis=-1, keepdims=True)


def _sortable_key(score):
    score = jnp.where(score == 0.0, 0.0, score)
    bits = pltpu.bitcast(score, jnp.int32)
    return bits ^ ((bits >> 31) & jnp.int32(0x7FFFFFFF))


def _topk_select(key_ref, n_sel, n_cols):
    rows = key_ref.shape[0]
    want = jnp.float32(n_sel)

    def count_ge(cand):
        return jnp.sum(jnp.where(key_ref[...] >= cand, 1.0, 0.0), axis=-1, keepdims=True)

    int_min = jnp.int32(-2 ** 31)
    t0 = jnp.full((rows, 1), int_min, jnp.int32)
    cand0 = jnp.zeros((rows, 1), jnp.int32)
    t0 = jnp.where(count_ge(cand0) >= want, cand0, t0)

    def thr_body(i, t):
        cand = t + jnp.left_shift(jnp.int32(1), 30 - i)
        return jnp.where(count_ge(cand) >= want, cand, t)

    thr = lax.fori_loop(0, 31, thr_body, t0)

    key = key_ref[...]
    greater = key > thr
    tie = key == thr
    need = want - jnp.sum(jnp.where(greater, 1.0, 0.0), axis=-1, keepdims=True)
    col = lax.broadcasted_iota(jnp.int32, key.shape, 1)

    n_bits = max(1, (n_cols - 1).bit_length())

    def tie_body(i, jj):
        cand = jj + jnp.left_shift(jnp.int32(1), n_bits - 1 - i)
        cnt = jnp.sum(jnp.where((key_ref[...] == thr) & (col < cand), 1.0, 0.0),
                      axis=-1, keepdims=True)
        return jnp.where(cnt < need, cand, jj)

    jmax = lax.fori_loop(0, n_bits, tie_body, jnp.zeros((rows, 1), jnp.int32))
    return greater | (tie & (col <= jmax))


def _sb_prompt_kernel(q_ref, k_ref, v_ref, o_ref, *, blk, scale):
    qi = pl.program_id(2)
    q = q_ref[0]
    row = lax.broadcasted_iota(jnp.int32, (blk, blk), 0)
    col = lax.broadcasted_iota(jnp.int32, (blk, blk), 1)

    def cond(carry):
        i, live, _, _ = carry
        return jnp.logical_and(i <= qi, live > 0)

    def body(carry):
        i, _, after_block, acc = carry
        kb = qi - i
        start = pl.multiple_of(kb * blk, blk)
        k = k_ref[0, pl.ds(start, blk), :].astype(BF16)
        v = v_ref[0, pl.ds(start, blk), :].astype(BF16)
        z = lax.dot_general(q, k, NT_DIMS, preferred_element_type=F32) * scale
        allowed = (kb * blk + col) < (qi * blk + row)
        w, keep_sum = _stick_break_block(z, allowed, after_block)
        acc = acc + jnp.dot(w.astype(BF16), v, preferred_element_type=F32)
        after_block = after_block + keep_sum
        live = (jnp.max(after_block) > EXP_UNDERFLOW).astype(jnp.int32)
        return i + 1, live, after_block, acc

    init = (jnp.int32(0), jnp.int32(1), jnp.zeros((blk, 1), F32), jnp.zeros((blk, HEAD_DIM), F32))
    _, _, _, acc = lax.while_loop(cond, body, init)
    o_ref[0] = acc.astype(o_ref.dtype)


def sb_prompt(q, k, v, blk=256):
    b, t, w = q.shape
    h = w // HEAD_DIM
    blk = min(blk, t)
    return pl.pallas_call(
        functools.partial(_sb_prompt_kernel, blk=blk, scale=HEAD_DIM ** -0.5),
        grid=(b, h, t // blk),
        in_specs=[pl.BlockSpec((1, blk, HEAD_DIM), lambda bi, hi, qi: (bi, qi, hi)),
                  pl.BlockSpec((1, t, HEAD_DIM), lambda bi, hi, qi: (bi, 0, hi)),
                  pl.BlockSpec((1, t, HEAD_DIM), lambda bi, hi, qi: (bi, 0, hi))],
        out_specs=pl.BlockSpec((1, blk, HEAD_DIM), lambda bi, hi, qi: (bi, qi, hi)),
        out_shape=jax.ShapeDtypeStruct((b, t, w), BF16),
        compiler_params=_params("parallel", "parallel", "arbitrary"),
        name="sb_prompt",
    )(q, k, v)


def _dsa_prompt_kernel(qix_ref, kwq_ref, kwk_ref, q_ref, k_ref, v_ref, o_ref,
                       qstack_ref, wb_ref, ke_ref, ko_ref, key_ref, bias_ref,
                       *, bq, s_len, n_sel, n_pair, cchunk, wstep, idx_scale, scale):
    qi = pl.program_id(1)
    h = pl.program_id(2)
    q_end = (qi + 1) * bq
    widths = [(v + 1) * wstep for v in range(s_len // wstep)]

    def uses_width(width):
        return jnp.logical_and(q_end > width - wstep, q_end <= width)

    @pl.when(h == 0)
    def _():
        kw = kwk_ref[0]
        lane = lax.broadcasted_iota(jnp.int32, kw.shape, 1)
        ke_ref[...] = jnp.where(lane < D_IDX, kw, 0.0).astype(BF16)
        ko_ref[...] = jnp.where(lane >= D_IDX, pltpu.roll(kw, D_IDX, 1), 0.0).astype(BF16)
        for p in range(n_pair):
            qstack_ref[p * bq:(p + 1) * bq, :] = qix_ref[0, :, p * LANES:(p + 1) * LANES]
        wq = kwq_ref[0]
        for i in range(2 * n_pair):
            wb_ref[i * bq:(i + 1) * bq, :] = jnp.broadcast_to(
                wq[:, D_IDX + i:D_IDX + i + 1], (bq, LANES))

        for c in range(s_len // cchunk):
            cols = slice(c * cchunk, (c + 1) * cchunk)

            @pl.when(c * cchunk < q_end)
            def _(c=c, cols=cols):
                qs = qstack_ref[...]
                r0 = lax.dot_general(qs, ke_ref[cols, :], NT_DIMS, preferred_element_type=F32)
                r1 = lax.dot_general(qs, ko_ref[cols, :], NT_DIMS, preferred_element_type=F32)
                sc = jnp.zeros((bq, cchunk), F32)
                for p in range(n_pair):
                    w0 = wb_ref[(2 * p) * bq:(2 * p + 1) * bq, :]
                    w1 = wb_ref[(2 * p + 1) * bq:(2 * p + 2) * bq, :]
                    if cchunk > LANES:
                        w0 = jnp.concatenate([w0] * (cchunk // LANES), axis=1)
                        w1 = jnp.concatenate([w1] * (cchunk // LANES), axis=1)
                    sc = sc + w0 * jnp.maximum(r0[p * bq:(p + 1) * bq, :], 0.0)
                    sc = sc + w1 * jnp.maximum(r1[p * bq:(p + 1) * bq, :], 0.0)
                rowq = lax.broadcasted_iota(jnp.int32, (bq, cchunk), 0) + qi * bq
                colk = lax.broadcasted_iota(jnp.int32, (bq, cchunk), 1) + c * cchunk
                sc = jnp.where(colk <= rowq, sc * idx_scale, -jnp.inf)
                key_ref[:, cols] = _sortable_key(sc)

            @pl.when(c * cchunk >= q_end)
            def _(cols=cols):
                key_ref[:, cols] = _sortable_key(jnp.full((bq, cchunk), -jnp.inf, F32))

        for width in widths:
            @pl.when(uses_width(width))
            def _(width=width):
                sel = _topk_select(key_ref.at[:, :width], n_sel, width)
                rowf = lax.broadcasted_iota(jnp.int32, (bq, width), 0) + qi * bq
                colf = lax.broadcasted_iota(jnp.int32, (bq, width), 1)
                bias_ref[:, :width] = jnp.where(sel & (colf <= rowf), 0.0, MASK_BIAS)

    for width in widths:
        @pl.when(uses_width(width))
        def _(width=width):
            q = q_ref[0]
            k = k_ref[0, :width, :].astype(BF16)
            v = v_ref[0, :width, :].astype(BF16)
            logits = (lax.dot_general(q, k, NT_DIMS, preferred_element_type=F32) * scale
                      + bias_ref[:, :width])
            m = jnp.max(logits, axis=-1, keepdims=True)
            p = jnp.exp(logits - m)
            denom = jnp.sum(p, axis=-1, keepdims=True)
            out = jnp.dot(p.astype(BF16), v, preferred_element_type=F32) / denom
            o_ref[0] = out.astype(o_ref.dtype)


def dsa_prompt(q_ix, kw, q, k, v, n_idx, bq=256):
    b, t, w = q.shape
    h = w // HEAD_DIM
    bq = min(bq, t)
    n_sel = min(TOPK_MAX, t // 4)
    n_pair = n_idx // 2
    cchunk = min(256, t)
    wstep = max(cchunk, t // 4)
    assert t % wstep == 0 and wstep % cchunk == 0 and n_sel <= wstep
    kern = functools.partial(
        _dsa_prompt_kernel, bq=bq, s_len=t, n_sel=n_sel, n_pair=n_pair, cchunk=cchunk,
        wstep=wstep, idx_scale=(D_IDX ** -0.5) * (n_idx ** -0.5), scale=HEAD_DIM ** -0.5)
    return pl.pallas_call(
        kern,
        grid=(b, t // bq, h),
        in_specs=[pl.BlockSpec((1, bq, n_idx * D_IDX), lambda bi, qi, hi: (bi, qi, 0)),
                  pl.BlockSpec((1, bq, LANES), lambda bi, qi, hi: (bi, qi, 0)),
                  pl.BlockSpec((1, t, LANES), lambda bi, qi, hi: (bi, 0, 0)),
                  pl.BlockSpec((1, bq, HEAD_DIM), lambda bi, qi, hi: (bi, qi, hi)),
                  pl.BlockSpec((1, t, HEAD_DIM), lambda bi, qi, hi: (bi, 0, hi)),
                  pl.BlockSpec((1, t, HEAD_DIM), lambda bi, qi, hi: (bi, 0, hi))],
        out_specs=pl.BlockSpec((1, bq, HEAD_DIM), lambda bi, qi, hi: (bi, qi, hi)),
        out_shape=jax.ShapeDtypeStruct((b, t, w), BF16),
        scratch_shapes=[pltpu.VMEM((n_pair * bq, LANES), BF16),
                        pltpu.VMEM((2 * n_pair * bq, LANES), F32),
                        pltpu.VMEM((t, LANES), BF16),
                        pltpu.VMEM((t, LANES), BF16),
                        pltpu.VMEM((bq, t), jnp.int32),
                        pltpu.VMEM((bq, t), F32)],
        compiler_params=_params("parallel", "parallel", "arbitrary"),
        name="dsa_prompt",
    )(q_ix, kw, kw, q, k, v)


def _head_group(n_heads):
    return 4 if n_heads % 4 == 0 else (2 if n_heads % 2 == 0 else 1)


def _regroup_rows(ref, slab_ref, n_heads, page):
    g = slab_ref.shape[0]
    for r in range(g):
        slab_ref[r] = ref[0, pl.ds(r, page * n_heads // g, stride=g), :]


def _head_rows(slab_ref, hh, n_heads, page):
    g = slab_ref.shape[0]
    return slab_ref[hh % g, pl.ds(hh // g, page, stride=n_heads // g), :].astype(BF16)


def _head_logits(q_ref, k_ref, slab_ref, n_t, n_heads, page):
    _regroup_rows(k_ref, slab_ref, n_heads, page)
    zs = []
    for hh in range(n_heads):
        z = lax.dot_general(q_ref[0, hh], _head_rows(slab_ref, hh, n_heads, page), NT_DIMS,
                            preferred_element_type=F32)
        zs.append(z[:n_t])
    return jnp.concatenate(zs, axis=0)


def _head_weighted_values(w, v_ref, slab_ref, n_t, n_heads, page, q_rows):
    _regroup_rows(v_ref, slab_ref, n_heads, page)
    outs = []
    pad = jnp.zeros((q_rows - n_t, page), F32)
    for hh in range(n_heads):
        w_h = jnp.concatenate([w[hh * n_t:(hh + 1) * n_t], pad], axis=0).astype(BF16)
        o = jnp.dot(w_h, _head_rows(slab_ref, hh, n_heads, page), preferred_element_type=F32)
        outs.append(o[:n_t])
    return outs


def _sb_sample_kernel(pt_ref, live_in_ref, q_ref, kn_ref, vn_ref, kp_ref, vp_ref, acc_in_ref,
                      after_in_ref, acc_ref, after_ref, live_ref, slab_ref,
                      *, with_new, n_t, n_heads, page, scale):
    bi = pl.program_id(0)
    p = pl.program_id(1)
    q_rows = q_ref.shape[2]

    def step(k_ref, v_ref, is_new):
        z = _head_logits(q_ref, k_ref, slab_ref, n_t, n_heads, page) * scale
        allowed = None
        if is_new:
            row = lax.broadcasted_iota(jnp.int32, z.shape, 0)
            col = lax.broadcasted_iota(jnp.int32, z.shape, 1)
            allowed = col < (row % n_t)
        w, keep_sum = _stick_break_block(z, allowed, after_ref[0])
        for hh, o in enumerate(
                _head_weighted_values(w, v_ref, slab_ref, n_t, n_heads, page, q_rows)):
            acc_ref[0, hh] += o
        after = after_ref[0] + keep_sum
        after_ref[0] = after
        live_ref[0] = (jnp.max(after) > EXP_UNDERFLOW).astype(jnp.int32)

    @pl.when(p == 0)
    def _():
        acc_ref[...] = acc_in_ref[...]
        after_ref[...] = after_in_ref[...]
        live_ref[0] = live_in_ref[bi]

    if with_new:
        @pl.when(p == 0)
        def _():
            step(kn_ref, vn_ref, True)

        @pl.when(jnp.logical_and(p > 0, live_ref[0] > 0))
        def _():
            step(kp_ref, vp_ref, False)
    else:
        @pl.when(live_ref[0] > 0)
        def _():
            step(kp_ref, vp_ref, False)


def sb_sample_phase(page_table, live, state, q_heads, k_new, v_new, cache_k, cache_v, layer, n_t,
                    first_rank, n_page_steps, with_new):
    db, n_heads, q_rows, _ = q_heads.shape
    n_pages = page_table.shape[1]
    rows = cache_k.shape[2]
    page = rows // n_heads
    acc_in, after_in = state

    def page_map(bi, pi, pt, lv):
        step = jnp.maximum(pi - 1, 0) if with_new else pi
        rank = first_rank + jnp.where(lv[bi] > 0, step, 0)
        return (layer, pt[bi, n_pages - 1 - rank], 0, 0)

    const = lambda nd: (lambda bi, pi, pt, lv: (bi,) + (0,) * (nd - 1))
    state_specs = [pl.BlockSpec((1, n_heads, n_t, HEAD_DIM), const(4)),
                   pl.BlockSpec((1, n_heads * n_t, 1), const(3))]
    grid_spec = pltpu.PrefetchScalarGridSpec(
        num_scalar_prefetch=2,
        grid=(db, n_page_steps + (1 if with_new else 0)),
        in_specs=[pl.BlockSpec((1, n_heads, q_rows, HEAD_DIM), const(4)),
                  pl.BlockSpec((1, rows, HEAD_DIM), const(3)),
                  pl.BlockSpec((1, rows, HEAD_DIM), const(3)),
                  pl.BlockSpec((None, 1, rows, HEAD_DIM), page_map),
                  pl.BlockSpec((None, 1, rows, HEAD_DIM), page_map),
                  *state_specs],
        out_specs=state_specs,
        scratch_shapes=[pltpu.SMEM((1,), jnp.int32),
                        pltpu.VMEM((_head_group(n_heads), rows // _head_group(n_heads), HEAD_DIM),
                                   F32)],
    )
    return pl.pallas_call(
        functools.partial(_sb_sample_kernel, with_new=with_new, n_t=n_t, n_heads=n_heads,
                          page=page, scale=HEAD_DIM ** -0.5),
        grid_spec=grid_spec,
        out_shape=(jax.ShapeDtypeStruct(acc_in.shape, F32),
                   jax.ShapeDtypeStruct(after_in.shape, F32)),
        compiler_params=_params("parallel", "arbitrary"),
        name="sb_sample_new" if with_new else "sb_sample_old",
    )(page_table, live, q_heads, k_new, v_new, cache_k, cache_v, acc_in, after_in)


def sb_sample(page_table, q_heads, k_new, v_new, cache_k, cache_v, layer, n_t, first_pages=3):
    db, n_heads, _, _ = q_heads.shape
    n_pages = page_table.shape[1]
    first_pages = min(first_pages, n_pages - 1)
    state = (jnp.zeros((db, n_heads, n_t, HEAD_DIM), F32), jnp.zeros((db, n_heads * n_t, 1), F32))
    args = (q_heads, k_new, v_new, cache_k, cache_v, layer, n_t)
    state = sb_sample_phase(page_table, jnp.ones((db,), jnp.int32), state, *args,
                            first_rank=0, n_page_steps=first_pages, with_new=True)
    live = (jnp.max(state[1], axis=(1, 2)) > EXP_UNDERFLOW).astype(jnp.int32)
    acc, _ = sb_sample_phase(page_table, live, state, *args, first_rank=first_pages,
                             n_page_steps=n_pages - first_pages, with_new=False)
    return acc.transpose(0, 2, 1, 3).reshape(db, n_t, n_heads * HEAD_DIM)


def _idx_sample_kernel(pt_ref, q_ref, wb_ref, kn_ref, *refs, n_t, n_idx, page, idx_scale):
    kp_refs, o_ref = refs[:-1], refs[-1]
    p = pl.program_id(1)
    last = pl.num_programs(1) - 1

    def score(k32):
        rel = lax.dot_general(q_ref[0], k32.astype(BF16), NT_DIMS, preferred_element_type=F32)
        rel = jnp.maximum(rel, 0.0) * wb_ref[0]
        return jnp.sum(rel.reshape(n_idx, n_t, rel.shape[1]), axis=0) * idx_scale

    @pl.when(p < last)
    def _():
        for g, kp_ref in enumerate(kp_refs):
            o_ref[0, :, g * page:(g + 1) * page] = score(kp_ref[0])

    @pl.when(p == last)
    def _():
        sc = score(kn_ref[0])
        row = lax.broadcasted_iota(jnp.int32, sc.shape, 0)
        col = lax.broadcasted_iota(jnp.int32, sc.shape, 1)
        o_ref[0] = jnp.full(o_ref.shape[1:], -jnp.inf, F32)
        o_ref[0, :, :page] = jnp.where(col <= row, sc, -jnp.inf)


def idx_sample(page_table, q_rows, w_rows, k_new, cache_ik, layer, n_t, n_idx):
    db, rows, _ = q_rows.shape
    n_pages = page_table.shape[1]
    page = cache_ik.shape[2]
    group = max(g for g in (8, 4, 2, 1) if n_pages % g == 0)
    n_groups = n_pages // group

    def page_map(g):
        def index(bi, pi, pt):
            return (layer, pt[bi, jnp.minimum(pi, n_groups - 1) * group + g], 0, 0)
        return index

    grid_spec = pltpu.PrefetchScalarGridSpec(
        num_scalar_prefetch=1,
        grid=(db, n_groups + 1),
        in_specs=[pl.BlockSpec((1, rows, D_IDX), lambda bi, pi, pt: (bi, 0, 0)),
                  pl.BlockSpec((1, rows, page), lambda bi, pi, pt: (bi, 0, 0)),
                  pl.BlockSpec((1, page, D_IDX), lambda bi, pi, pt: (bi, 0, 0)),
                  *[pl.BlockSpec((None, 1, page, D_IDX), page_map(g)) for g in range(group)]],
        out_specs=pl.BlockSpec((1, n_t, group * page), lambda bi, pi, pt: (bi, 0, pi)),
    )
    return pl.pallas_call(
        functools.partial(_idx_sample_kernel, n_t=n_t, n_idx=n_idx, page=page,
                          idx_scale=(D_IDX ** -0.5) * (n_idx ** -0.5)),
        grid_spec=grid_spec,
        out_shape=jax.ShapeDtypeStruct((db, n_t, (n_groups + 1) * group * page), F32),
        compiler_params=_params("parallel", "arbitrary"),
        name="idx_sample",
    )(page_table, q_rows, w_rows, k_new, *([cache_ik] * group))


def _select_sample_kernel(s_ref, o_ref, key_ref, *, n_sel, n_cols):
    sc = s_ref[0]
    key_ref[...] = _sortable_key(sc)
    sel = _topk_select(key_ref, n_sel, n_cols)
    o_ref[0] = jnp.where(sel & (sc > -jnp.inf), 0.0, MASK_BIAS)


def select_sample(scores, n_sel):
    db, n_t, c = scores.shape
    return pl.pallas_call(
        functools.partial(_select_sample_kernel, n_sel=n_sel, n_cols=c),
        grid=(db,),
        in_specs=[pl.BlockSpec((1, n_t, c), lambda bi: (bi, 0, 0))],
        out_specs=pl.BlockSpec((1, n_t, c), lambda bi: (bi, 0, 0)),
        out_shape=jax.ShapeDtypeStruct((db, n_t, c), F32),
        scratch_shapes=[pltpu.VMEM((n_t, c), jnp.int32)],
        compiler_params=_params("parallel"),
        name="select_sample",
    )(scores)


def _dsa_sample_kernel(pt_ref, q_ref, bias_ref, kn_ref, vn_ref, kp_ref, vp_ref, o_ref,
                       acc_ref, m_ref, l_ref, slab_ref, *, n_t, n_heads, page, scale):
    p = pl.program_id(1)
    q_rows = q_ref.shape[2]

    def step(k_ref, v_ref):
        bias = jnp.concatenate([bias_ref[0]] * n_heads, axis=0)
        logits = _head_logits(q_ref, k_ref, slab_ref, n_t, n_heads, page) * scale + bias
        m_old = m_ref[...]
        m_new = jnp.maximum(m_old, jnp.max(logits, axis=-1, keepdims=True))
        alpha = jnp.exp(m_old - m_new)
        pr = jnp.where(bias == 0.0, jnp.exp(logits - m_new), 0.0)
        l_ref[...] = alpha * l_ref[...] + jnp.sum(pr, axis=-1, keepdims=True)
        for hh, o in enumerate(
                _head_weighted_values(pr, v_ref, slab_ref, n_t, n_heads, page, q_rows)):
            acc_ref[hh] = alpha[hh * n_t:(hh + 1) * n_t] * acc_ref[hh] + o
        m_ref[...] = m_new

    @pl.when(p == 0)
    def _():
        acc_ref[...] = jnp.zeros_like(acc_ref)
        l_ref[...] = jnp.zeros_like(l_ref)
        m_ref[...] = jnp.full_like(m_ref, MASK_BIAS)
        step(kn_ref, vn_ref)

    @pl.when(p > 0)
    def _():
        step(kp_ref, vp_ref)

    @pl.when(p == pl.num_programs(1) - 1)
    def _():
        inv = 1.0 / l_ref[...]
        for hh in range(n_heads):
            o_ref[0, :, hh * HEAD_DIM:(hh + 1) * HEAD_DIM] = (
                acc_ref[hh] * inv[hh * n_t:(hh + 1) * n_t])


def dsa_sample(page_table, q_heads, bias, k_new, v_new, cache_k, cache_v, layer, n_t):
    db, n_heads, q_rows, _ = q_heads.shape
    n_pages = page_table.shape[1]
    rows = cache_k.shape[2]
    page = rows // n_heads
    w = n_heads * HEAD_DIM

    def page_map(bi, pi, pt):
        return (layer, pt[bi, jnp.maximum(pi, 1) - 1], 0, 0)

    def bias_map(bi, pi, pt):
        return (bi, 0, jnp.where(pi == 0, n_pages, pi - 1))

    grid_spec = pltpu.PrefetchScalarGridSpec(
        num_scalar_prefetch=1,
        grid=(db, n_pages + 1),
        in_specs=[pl.BlockSpec((1, n_heads, q_rows, HEAD_DIM), lambda bi, pi, pt: (bi, 0, 0, 0)),
                  pl.BlockSpec((1, n_t, page), bias_map),
                  pl.BlockSpec((1, rows, HEAD_DIM), lambda bi, pi, pt: (bi, 0, 0)),
                  pl.BlockSpec((1, rows, HEAD_DIM), lambda bi, pi, pt: (bi, 0, 0)),
                  pl.BlockSpec((None, 1, rows, HEAD_DIM), page_map),
                  pl.BlockSpec((None, 1, rows, HEAD_DIM), page_map)],
        out_specs=pl.BlockSpec((1, n_t, w), lambda bi, pi, pt: (bi, 0, 0)),
        scratch_shapes=[pltpu.VMEM((n_heads, n_t, HEAD_DIM), F32),
                        pltpu.VMEM((n_heads * n_t, 1), F32),
                        pltpu.VMEM((n_heads * n_t, 1), F32),
                        pltpu.VMEM((_head_group(n_heads), rows // _head_group(n_heads), HEAD_DIM),
                                   F32)],
    )
    return pl.pallas_call(
        functools.partial(_dsa_sample_kernel, n_t=n_t, n_heads=n_heads, page=page,
                          scale=HEAD_DIM ** -0.5),
        grid_spec=grid_spec,
        out_shape=jax.ShapeDtypeStruct((db, n_t, w), F32),
        compiler_params=_params("parallel", "arbitrary"),
        name="dsa_sample",
    )(page_table, q_heads, bias, k_new, v_new, cache_k, cache_v)


def _rope_tables(pos, d, reps_rows):
    inv = ROPE_THETA ** (-jnp.arange(0, d, 2, dtype=F32) / d)
    ang = pos.astype(F32)[:, None] * inv[None, :]
    cos = jnp.cos(ang)
    sin = jnp.sin(ang)
    cosf = jnp.tile(jnp.concatenate([cos, cos], axis=-1), (reps_rows, LANES // d))
    sinf = jnp.tile(jnp.concatenate([-sin, sin], axis=-1), (reps_rows, LANES // d))
    return cosf, sinf


def _head_major_queries(q, n_heads, q_rows):
    db, n_t, _ = q.shape
    q4 = q.reshape(db, n_t, n_heads, HEAD_DIM).transpose(0, 2, 1, 3)
    return jnp.pad(q4, ((0, 0), (0, 0), (0, q_rows - n_t), (0, 0)))


def _new_rows_as_page(x, n_heads, page):
    db, n_t, _ = x.shape
    x = jnp.pad(x, ((0, 0), (0, page - n_t), (0, 0)))
    return x.reshape(db, page * n_heads, HEAD_DIM)


def _pad_rows(x, rows):
    return jnp.pad(x, ((0, 0), (0, rows - x.shape[1]), (0, 0)))


def kernel(x_prompt, x_sample, cache_sb_k, cache_sb_v, cache_dsa_k, cache_dsa_v, cache_idx_k,
           page_table, norm_mix, w_in, w_branch_sb, w_branch_dsa, w_out, norm_mlp, w_up, w_down,
           norm_final):
    bsz, seq, d_model = x_prompt.shape
    db, n_t, _ = x_sample.shape
    depth, n_pool, page, h_sb, _ = cache_sb_k.shape
    h_dsa = cache_dsa_k.shape[3]
    n_pages = page_table.shape[1]
    past_len = n_pages * page
    sb_w = h_sb * HEAD_DIM
    dsa_w = h_dsa * HEAD_DIM
    n_in = w_in.shape[2]
    n_idx = (n_in - 3 * sb_w - 3 * dsa_w - D_IDX - 2 * d_model) // (D_IDX + 1)
    idx_w = n_idx * D_IDX
    c_qsb, c_ksb, c_vsb = 0, sb_w, 2 * sb_w
    c_qds, c_kds, c_vds = 3 * sb_w, 3 * sb_w + dsa_w, 3 * sb_w + 2 * dsa_w
    c_qix = 3 * sb_w + 3 * dsa_w
    c_kw = c_qix + idx_w
    c_gate = c_kw + D_IDX + n_idx
    assert D_IDX + n_idx <= LANES and n_t <= page

    w_main = column_blocks(w_in[:, :, :c_kw].astype(BF16), PROJ_TN)
    w_kw = jnp.pad(w_in[:, :, c_kw:c_gate], ((0, 0), (0, 0), (0, LANES - (c_gate - c_kw)))).astype(BF16)
    w_gate = column_blocks(w_in[:, :, c_gate:].astype(BF16), PROJ_TN)
    w_bsb = w_branch_sb.astype(BF16)
    w_bds = w_branch_dsa.astype(BF16)
    w_o = column_blocks(w_out.astype(BF16), PROJ_TN)
    w_u = column_blocks(w_up.astype(BF16), PROJ_TN)
    w_d = w_down.astype(BF16)

    pos_p = jnp.arange(seq, dtype=jnp.int32)
    pos_s = past_len + jnp.arange(n_t, dtype=jnp.int32)
    tables = {}
    for name, pos, reps in (("p", pos_p, bsz), ("s", pos_s, db)):
        cos128, sin128 = _rope_tables(pos, HEAD_DIM, reps)
        cos64, sin64 = _rope_tables(pos, D_IDX, reps)
        lane = jnp.arange(LANES)[None, :]
        cos_kw = jnp.where(lane < D_IDX, cos64, 1.0)
        sin_kw = jnp.where(lane < D_IDX, sin64, 0.0)
        tables[name] = (cos128, sin128, cos64, sin64, cos_kw, sin_kw)

    cache_sb_k2 = cache_sb_k.reshape(depth, n_pool, page * h_sb, HEAD_DIM)
    cache_sb_v2 = cache_sb_v.reshape(depth, n_pool, page * h_sb, HEAD_DIM)
    cache_ds_k2 = cache_dsa_k.reshape(depth, n_pool, page * h_dsa, HEAD_DIM)
    cache_ds_v2 = cache_dsa_v.reshape(depth, n_pool, page * h_dsa, HEAD_DIM)
    q_rows_pad = -(-n_t // BF16_ROWS) * BF16_ROWS

    def project(xn, layer, tbl):
        cos128, sin128, cos64, sin64, cos_kw, sin_kw = tbl
        m = xn.shape[0]
        tm = min(PROJ_TM, m)
        rope_specs = (_row_spec(tm, LANES), _row_spec(tm, LANES))
        mm = functools.partial(matmul, xn, tm=tm, tk=d_model)
        q_sb = mm(w_main, layer, c_qsb, sb_w, BF16)
        k_sb, k_sb16 = mm(w_main, layer, c_ksb, sb_w, F32, copy_dtype=BF16)
        v_sb, v_sb16 = mm(w_main, layer, c_vsb, sb_w, F32, copy_dtype=BF16)
        q_ds = mm(w_main, layer, c_qds, dsa_w, BF16, _ep_rope128, (cos128, sin128), rope_specs)
        k_ds, k_ds16 = mm(w_main, layer, c_kds, dsa_w, F32, _ep_rope128, (cos128, sin128),
                          rope_specs, copy_dtype=BF16)
        v_ds, v_ds16 = mm(w_main, layer, c_vds, dsa_w, F32, copy_dtype=BF16)
        q_ix = mm(w_main, layer, c_qix, idx_w, BF16, _ep_rope64, (cos64, sin64), rope_specs)
        kw = mm(w_kw, layer, 0, LANES, F32, _ep_rope64, (cos_kw, sin_kw), rope_specs)
        gates = mm(w_gate, layer, 0, 2 * d_model, BF16, _ep_sigmoid)
        kv16 = (k_sb16, v_sb16, k_ds16, v_ds16)
        return q_sb, k_sb, v_sb, q_ds, k_ds, v_ds, q_ix, kw, gates, kv16

    def finish_layer(h, o_sb, o_ds, gates, layer):
        m = h.shape[0]
        tm = min(1024, m)
        merged = merge_branches(o_sb, o_ds, w_bsb, w_bds, gates, layer, d_model, tm=tm)
        tm_proj = min(PROJ_TM, m)
        h = matmul(merged, w_o, layer, 0, d_model, F32, _ep_residual, (h,),
                   (_tile_spec(tm_proj, PROJ_TN),), tm=tm_proj, tk=d_model)
        xn = rmsnorm(h, norm_mlp[layer], BF16)
        hid = matmul(xn, w_u, layer, 0, w_up.shape[2], BF16, _ep_relu2, tm=tm_proj, tk=d_model)
        tn_down = min(DOWN_TN, d_model)
        h = matmul(hid, w_d, layer, 0, d_model, F32, _ep_residual, (h,), (_tile_spec(tm, tn_down),),
                   tm=tm, tn=tn_down, tk=2048)
        return h

    hp = x_prompt.reshape(bsz * seq, d_model)
    hs = x_sample.reshape(db * n_t, d_model)
    rows_p = [[] for _ in range(5)]
    rows_s = [[] for _ in range(5)]
    n_sel_s = min(TOPK_MAX, (past_len + n_t) // 4)

    for layer in range(depth):
        xn = rmsnorm(hp, norm_mix[layer], BF16)
        q_sb, k_sb, v_sb, q_ds, k_ds, v_ds, q_ix, kw, gates, kv16 = project(
            xn, layer, tables["p"])
        r3 = lambda a: a.reshape(bsz, seq, a.shape[1])
        o_sb = sb_prompt(r3(q_sb), r3(kv16[0]), r3(kv16[1]))
        o_ds = dsa_prompt(r3(q_ix), r3(kw), r3(q_ds), r3(kv16[2]), r3(kv16[3]), n_idx)
        hp = finish_layer(hp, o_sb.reshape(bsz * seq, sb_w), o_ds.reshape(bsz * seq, dsa_w),
                          gates, layer)
        for lst, r in zip(rows_p, (k_sb, v_sb, k_ds, v_ds, kw[:, :D_IDX])):
            lst.append(r)

        xn = rmsnorm(hs, norm_mix[layer], BF16)
        q_sb, k_sb, v_sb, q_ds, k_ds, v_ds, q_ix, kw, gates, _ = project(xn, layer, tables["s"])
        s3 = lambda a: a.reshape(db, n_t, a.shape[1])
        o_sb = sb_sample(page_table, _head_major_queries(s3(q_sb), h_sb, q_rows_pad),
                         _new_rows_as_page(s3(k_sb), h_sb, page),
                         _new_rows_as_page(s3(v_sb), h_sb, page),
                         cache_sb_k2, cache_sb_v2, layer, n_t)
        kw3 = s3(kw)
        q_rows = s3(q_ix).reshape(db, n_t, n_idx, D_IDX).transpose(0, 2, 1, 3).reshape(
            db, n_idx * n_t, D_IDX)
        w_rows = jnp.broadcast_to(
            kw3[:, :, D_IDX:D_IDX + n_idx].transpose(0, 2, 1).reshape(db, n_idx * n_t, 1),
            (db, n_idx * n_t, page))
        scores = idx_sample(page_table, q_rows, w_rows, _pad_rows(kw3[:, :, :D_IDX], page),
                            cache_idx_k, layer, n_t, n_idx)
        bias = select_sample(scores, n_sel_s)
        o_ds = dsa_sample(page_table, _head_major_queries(s3(q_ds), h_dsa, q_rows_pad), bias,
                          _new_rows_as_page(s3(k_ds), h_dsa, page),
                          _new_rows_as_page(s3(v_ds), h_dsa, page),
                          cache_ds_k2, cache_ds_v2, layer, n_t)
        hs = finish_layer(hs, o_sb.reshape(db * n_t, sb_w).astype(BF16),
                          o_ds.reshape(db * n_t, dsa_w).astype(BF16), gates, layer)
        for lst, r in zip(rows_s, (k_sb, v_sb, k_ds, v_ds, kw[:, :D_IDX])):
            lst.append(r)

    y_prompt = rmsnorm(hp, norm_final, F32).reshape(bsz, seq, d_model)
    y_sample = rmsnorm(hs, norm_final, F32).reshape(db, n_t, d_model)

    def stack(lst, lead, heads):
        a = jnp.stack(lst, axis=0)
        if heads is None:
            return a.reshape((depth,) + lead + (D_IDX,))
        return a.reshape((depth,) + lead + (heads, HEAD_DIM))

    outs_p = [stack(rows_p[0], (bsz, seq), h_sb), stack(rows_p[1], (bsz, seq), h_sb),
              stack(rows_p[2], (bsz, seq), h_dsa), stack(rows_p[3], (bsz, seq), h_dsa),
              stack(rows_p[4], (bsz, seq), None)]
    outs_s = [stack(rows_s[0], (db, n_t), h_sb), stack(rows_s[1], (db, n_t), h_sb),
              stack(rows_s[2], (db, n_t), h_dsa), stack(rows_s[3], (db, n_t), h_dsa),
              stack(rows_s[4], (db, n_t), None)]
    return (y_prompt, y_sample, *outs_p, *outs_s)
```

```python
import functools

import jax
import jax.numpy as jnp
from jax import lax
from jax.experimental import pallas as pl
from jax.experimental.pallas import tpu as pltpu

HEAD_DIM = 128
D_IDX = 64
TOPK_MAX = 256
ROPE_THETA = 10000.0
EPS = 1e-6
LANES = 128
BF16_ROWS = 16
MASK_BIAS = -1e30
EXP_UNDERFLOW = -104.0
VMEM_LIMIT = 60000 * 1024
PROJ_TM = 2048
PROJ_TN = 256
DOWN_TN = 1024

F32 = jnp.float32
BF16 = jnp.bfloat16
NT_DIMS = (((1,), (1,)), ((), ()))


def _params(*sem):
    return pltpu.CompilerParams(dimension_semantics=sem, vmem_limit_bytes=VMEM_LIMIT)


def _rmsnorm_kernel(x_ref, g_ref, o_ref):
    x = x_ref[...]
    ms = jnp.mean(x * x, axis=-1, keepdims=True)
    o_ref[...] = (x * lax.rsqrt(ms + EPS) * g_ref[...]).astype(o_ref.dtype)


def rmsnorm(x, g, out_dtype):
    m, d = x.shape
    tm = min(256, m)
    return pl.pallas_call(
        _rmsnorm_kernel,
        grid=(m // tm,),
        in_specs=[pl.BlockSpec((tm, d), lambda i: (i, 0)),
                  pl.BlockSpec((1, d), lambda i: (0, 0))],
        out_specs=pl.BlockSpec((tm, d), lambda i: (i, 0)),
        out_shape=jax.ShapeDtypeStruct((m, d), out_dtype),
        compiler_params=_params("parallel"),
        name="rmsnorm",
    )(x, g.reshape(1, d))


def _ep_store(acc, o_ref):
    o_ref[...] = acc.astype(o_ref.dtype)


def _ep_relu2(acc, o_ref):
    o_ref[...] = jnp.square(jnp.maximum(acc, 0.0)).astype(o_ref.dtype)


def _ep_sigmoid(acc, o_ref):
    o_ref[...] = jax.nn.sigmoid(acc).astype(o_ref.dtype)


def _ep_residual(acc, o_ref, res_ref):
    o_ref[...] = (res_ref[...] + acc).astype(o_ref.dtype)


def _ep_rope128(acc, o_ref, cos_ref, sin_ref):
    cos = cos_ref[...]
    sin = sin_ref[...]
    for c in range(acc.shape[1] // LANES):
        x = acc[:, c * LANES:(c + 1) * LANES]
        y = x * cos + pltpu.roll(x, LANES // 2, 1) * sin
        o_ref[:, c * LANES:(c + 1) * LANES] = y.astype(o_ref.dtype)


def _ep_rope64(acc, o_ref, cos_ref, sin_ref):
    cos = cos_ref[...]
    sin = sin_ref[...]
    lane = lax.broadcasted_iota(jnp.int32, cos.shape, 1)
    first_half = (lane % D_IDX) < (D_IDX // 2)
    for c in range(acc.shape[1] // LANES):
        x = acc[:, c * LANES:(c + 1) * LANES]
        partner = jnp.where(first_half, pltpu.roll(x, LANES - D_IDX // 2, 1),
                            pltpu.roll(x, D_IDX // 2, 1))
        y = x * cos + partner * sin
        o_ref[:, c * LANES:(c + 1) * LANES] = y.astype(o_ref.dtype)


def _mm_kernel(*refs, nk, n_extra, epilogue, copy_dtype):
    x_ref, w_ref = refs[0], refs[1]
    extra = refs[2:2 + n_extra]
    o_ref = refs[2 + n_extra]
    n_out = 1 if copy_dtype is None else 2

    def finish(acc):
        epilogue(acc, o_ref, *extra)
        if copy_dtype is not None:
            refs[3 + n_extra][...] = o_ref[...].astype(copy_dtype)

    if nk == 1:
        finish(jnp.dot(x_ref[...], w_ref[...], preferred_element_type=F32))
        return
    acc_ref = refs[2 + n_extra + n_out]
    k = pl.program_id(2)

    @pl.when(k == 0)
    def _():
        acc_ref[...] = jnp.zeros_like(acc_ref)

    acc_ref[...] += jnp.dot(x_ref[...], w_ref[...], preferred_element_type=F32)

    @pl.when(k == nk - 1)
    def _():
        finish(acc_ref[...])


def column_blocks(w, tn):
    d, k, n = w.shape
    return w.reshape(d, k, n // tn, tn).transpose(0, 2, 1, 3)


def matmul(x, w, layer, col0, n, out_dtype, epilogue=_ep_store, extra=(), extra_specs=(),
           tm=1024, tn=512, tk=1024, copy_dtype=None):
    m, kdim = x.shape
    blocked = w.ndim == 4
    tm = min(tm, m)
    tn = w.shape[3] if blocked else min(tn, n)
    tk = min(tk, kdim)
    assert m % tm == 0 and n % tn == 0 and kdim % tk == 0 and col0 % tn == 0
    nk = kdim // tk
    off = col0 // tn
    if blocked:
        w_spec = pl.BlockSpec((None, None, tk, tn), lambda i, j, k: (layer, j + off, k, 0))
    else:
        w_spec = pl.BlockSpec((None, tk, tn), lambda i, j, k: (layer, k, j + off))
    scratch = [pltpu.VMEM((tm, tn), F32)] if nk > 1 else []
    out_spec = pl.BlockSpec((tm, tn), lambda i, j, k: (i, j))
    out_shape = jax.ShapeDtypeStruct((m, n), out_dtype)
    if copy_dtype is not None:
        out_spec = [out_spec, out_spec]
        out_shape = (out_shape, jax.ShapeDtypeStruct((m, n), copy_dtype))
    return pl.pallas_call(
        functools.partial(_mm_kernel, nk=nk, n_extra=len(extra), epilogue=epilogue,
                          copy_dtype=copy_dtype),
        grid=(m // tm, n // tn, nk),
        in_specs=[pl.BlockSpec((tm, tk), lambda i, j, k: (i, k)), w_spec, *extra_specs],
        out_specs=out_spec,
        out_shape=out_shape,
        scratch_shapes=scratch,
        compiler_params=_params("parallel", "parallel", "arbitrary"),
        name="matmul_" + epilogue.__name__[4:],
    )(x, w, *extra)


def _row_spec(tm, tn):
    return pl.BlockSpec((tm, tn), lambda i, j, k: (i, 0))


def _tile_spec(tm, tn):
    return pl.BlockSpec((tm, tn), lambda i, j, k: (i, j))


def _merge_kernel(a_ref, b_ref, wa_ref, wb_ref, ga_ref, gb_ref, o_ref):
    ya = jnp.dot(a_ref[...], wa_ref[...], preferred_element_type=F32)
    yb = jnp.dot(b_ref[...], wb_ref[...], preferred_element_type=F32)
    o_ref[...] = (ga_ref[...].astype(F32) * ya + gb_ref[...].astype(F32) * yb).astype(o_ref.dtype)


def merge_branches(o_sb, o_ds, w_sb, w_ds, gates, layer, d_model, tm=1024, tn=512):
    m, kdim = o_sb.shape
    tm = min(tm, m)
    nb = d_model // tn
    return pl.pallas_call(
        _merge_kernel,
        grid=(m // tm, nb),
        in_specs=[pl.BlockSpec((tm, kdim), lambda i, j: (i, 0)),
                  pl.BlockSpec((tm, kdim), lambda i, j: (i, 0)),
                  pl.BlockSpec((None, kdim, tn), lambda i, j: (layer, 0, j)),
                  pl.BlockSpec((None, kdim, tn), lambda i, j: (layer, 0, j)),
                  pl.BlockSpec((tm, tn), lambda i, j: (i, j)),
                  pl.BlockSpec((tm, tn), lambda i, j: (i, j + nb))],
        out_specs=pl.BlockSpec((tm, tn), lambda i, j: (i, j)),
        out_shape=jax.ShapeDtypeStruct((m, d_model), BF16),
        compiler_params=_params("parallel", "parallel"),
        name="merge_branches",
    )(o_sb, o_ds, w_sb, w_ds, gates, gates)


def _later_sum_matrix(n):
    j = lax.broadcasted_iota(jnp.int32, (n, n), 0)
    s = lax.broadcasted_iota(jnp.int32, (n, n), 1)
    return jnp.where(j > s, 1.0, 0.0).astype(BF16)


def _sum_later(x, u):
    hi = x.astype(BF16)
    lo = (x - hi.astype(F32)).astype(BF16)
    return (jnp.dot(hi, u, preferred_element_type=F32)
            + jnp.dot(lo, u, preferred_element_type=F32))


def _stick_break_block(z, allowed, after_block):
    softplus = jnp.log(1.0 + jnp.exp(-jnp.abs(z)))
    log_beta = jnp.minimum(z, 0.0) - softplus
    log_keep = log_beta - z
    if allowed is not None:
        log_keep = jnp.where(allowed, log_keep, 0.0)
    after = _sum_later(log_keep, _later_sum_matrix(z.shape[1])) + after_block
    w = jnp.exp(log_beta + after)
    if allowed is not None:
        w = jnp.where(allowed, w, 0.0)
    return w, jnp.sum(log_keep, axis=-1, keepdims=True)


def _sortable_key(score):
    score = jnp.where(score == 0.0, 0.0, score)
    bits = pltpu.bitcast(score, jnp.int32)
    return bits ^ ((bits >> 31) & jnp.int32(0x7FFFFFFF))


def _topk_select(key_ref, n_sel, n_cols):
    rows = key_ref.shape[0]
    want = jnp.float32(n_sel)

    def count_ge(cand):
        return jnp.sum(jnp.where(key_ref[...] >= cand, 1.0, 0.0), axis=-1, keepdims=True)

    int_min = jnp.int32(-2 ** 31)
    t0 = jnp.full((rows, 1), int_min, jnp.int32)
    cand0 = jnp.zeros((rows, 1), jnp.int32)
    t0 = jnp.where(count_ge(cand0) >= want, cand0, t0)

    def thr_body(i, t):
        cand = t + jnp.left_shift(jnp.int32(1), 30 - i)
        return jnp.where(count_ge(cand) >= want, cand, t)

    thr = lax.fori_loop(0, 31, thr_body, t0)

    key = key_ref[...]
    greater = key > thr
    tie = key == thr
    need = want - jnp.sum(jnp.where(greater, 1.0, 0.0), axis=-1, keepdims=True)
    col = lax.broadcasted_iota(jnp.int32, key.shape, 1)

    n_bits = max(1, (n_cols - 1).bit_length())
    any_tie = jnp.max(count_ge(thr)) > want

    def tie_body(i, jj):
        cand = jj + jnp.left_shift(jnp.int32(1), n_bits - 1 - i)
        cnt = jnp.sum(jnp.where((key_ref[...] == thr) & (col < cand), 1.0, 0.0),
                      axis=-1, keepdims=True)
        return jnp.where(cnt < need, cand, jj)

    jmax = lax.fori_loop(0, jnp.where(any_tie, n_bits, 0), tie_body,
                         jnp.full((rows, 1), jnp.where(any_tie, 0, n_cols), jnp.int32))
    return greater | (tie & (col <= jmax))


def _sb_prompt_kernel(q_ref, k_ref, v_ref, o_ref, *, blk, scale):
    qi = pl.program_id(2)
    q = q_ref[0]
    row = lax.broadcasted_iota(jnp.int32, (blk, blk), 0)
    col = lax.broadcasted_iota(jnp.int32, (blk, blk), 1)

    def cond(carry):
        i, live, _, _ = carry
        return jnp.logical_and(i <= qi, live > 0)

    def body(carry):
        i, _, after_block, acc = carry
        kb = qi - i
        start = pl.multiple_of(kb * blk, blk)
        k = k_ref[0, pl.ds(start, blk), :].astype(BF16)
        v = v_ref[0, pl.ds(start, blk), :].astype(BF16)
        z = lax.dot_general(q, k, NT_DIMS, preferred_element_type=F32) * scale
        allowed = (kb * blk + col) < (qi * blk + row)
        w, keep_sum = _stick_break_block(z, allowed, after_block)
        acc = acc + jnp.dot(w.astype(BF16), v, preferred_element_type=F32)
        after_block = after_block + keep_sum
        live = (jnp.max(after_block) > EXP_UNDERFLOW).astype(jnp.int32)
        return i + 1, live, after_block, acc

    init = (jnp.int32(0), jnp.int32(1), jnp.zeros((blk, 1), F32), jnp.zeros((blk, HEAD_DIM), F32))
    _, _, _, acc = lax.while_loop(cond, body, init)
    o_ref[0] = acc.astype(o_ref.dtype)


def sb_prompt(q, k, v, blk=256):
    b, t, w = q.shape
    h = w // HEAD_DIM
    blk = min(blk, t)
    return pl.pallas_call(
        functools.partial(_sb_prompt_kernel, blk=blk, scale=HEAD_DIM ** -0.5),
        grid=(b, h, t // blk),
        in_specs=[pl.BlockSpec((1, blk, HEAD_DIM), lambda bi, hi, qi: (bi, qi, hi)),
                  pl.BlockSpec((1, t, HEAD_DIM), lambda bi, hi, qi: (bi, 0, hi)),
                  pl.BlockSpec((1, t, HEAD_DIM), lambda bi, hi, qi: (bi, 0, hi))],
        out_specs=pl.BlockSpec((1, blk, HEAD_DIM), lambda bi, hi, qi: (bi, qi, hi)),
        out_shape=jax.ShapeDtypeStruct((b, t, w), BF16),
        compiler_params=_params("parallel", "parallel", "arbitrary"),
        name="sb_prompt",
    )(q, k, v)


def _dsa_prompt_kernel(qix_ref, kwq_ref, kwk_ref, q_ref, k_ref, v_ref, o_ref,
                       qstack_ref, wb_ref, ke_ref, ko_ref, key_ref, bias_ref,
                       *, bq, s_len, n_sel, n_pair, cchunk, wstep, idx_scale, scale):
    qi = pl.program_id(1)
    h = pl.program_id(2)
    q_end = (qi + 1) * bq
    widths = [(v + 1) * wstep for v in range(s_len // wstep)]

    def uses_width(width):
        return jnp.logical_and(q_end > width - wstep, q_end <= width)

    @pl.when(h == 0)
    def _():
        kw = kwk_ref[0]
        lane = lax.broadcasted_iota(jnp.int32, kw.shape, 1)
        ke_ref[...] = jnp.where(lane < D_IDX, kw, 0.0).astype(BF16)
        ko_ref[...] = jnp.where(lane >= D_IDX, pltpu.roll(kw, D_IDX, 1), 0.0).astype(BF16)
        for p in range(n_pair):
            qstack_ref[p * bq:(p + 1) * bq, :] = qix_ref[0, :, p * LANES:(p + 1) * LANES]
        wq = kwq_ref[0]
        for i in range(2 * n_pair):
            wb_ref[i * bq:(i + 1) * bq, :] = jnp.broadcast_to(
                wq[:, D_IDX + i:D_IDX + i + 1], (bq, LANES))

        for c in range(s_len // cchunk):
            cols = slice(c * cchunk, (c + 1) * cchunk)

            @pl.when(c * cchunk < q_end)
            def _(c=c, cols=cols):
                qs = qstack_ref[...]
                r0 = lax.dot_general(qs, ke_ref[cols, :], NT_DIMS, preferred_element_type=F32)
                r1 = lax.dot_general(qs, ko_ref[cols, :], NT_DIMS, preferred_element_type=F32)
                sc = jnp.zeros((bq, cchunk), F32)
                for p in range(n_pair):
                    w0 = wb_ref[(2 * p) * bq:(2 * p + 1) * bq, :]
                    w1 = wb_ref[(2 * p + 1) * bq:(2 * p + 2) * bq, :]
                    if cchunk > LANES:
                        w0 = jnp.concatenate([w0] * (cchunk // LANES), axis=1)
                        w1 = jnp.concatenate([w1] * (cchunk // LANES), axis=1)
                    sc = sc + w0 * jnp.maximum(r0[p * bq:(p + 1) * bq, :], 0.0)
                    sc = sc + w1 * jnp.maximum(r1[p * bq:(p + 1) * bq, :], 0.0)
                rowq = lax.broadcasted_iota(jnp.int32, (bq, cchunk), 0) + qi * bq
                colk = lax.broadcasted_iota(jnp.int32, (bq, cchunk), 1) + c * cchunk
                sc = jnp.where(colk <= rowq, sc * idx_scale, -jnp.inf)
                key_ref[:, cols] = _sortable_key(sc)

            @pl.when(c * cchunk >= q_end)
            def _(cols=cols):
                key_ref[:, cols] = _sortable_key(jnp.full((bq, cchunk), -jnp.inf, F32))

        for width in widths:
            @pl.when(uses_width(width))
            def _(width=width):
                sel = _topk_select(key_ref.at[:, :width], n_sel, width)
                rowf = lax.broadcasted_iota(jnp.int32, (bq, width), 0) + qi * bq
                colf = lax.broadcasted_iota(jnp.int32, (bq, width), 1)
                bias_ref[:, :width] = jnp.where(sel & (colf <= rowf), 0.0, MASK_BIAS)

    for width in widths:
        @pl.when(uses_width(width))
        def _(width=width):
            q = q_ref[0]
            k = k_ref[0, :width, :].astype(BF16)
            v = v_ref[0, :width, :].astype(BF16)
            logits = (lax.dot_general(q, k, NT_DIMS, preferred_element_type=F32) * scale
                      + bias_ref[:, :width])
            m = jnp.max(logits, axis=-1, keepdims=True)
            p = jnp.exp(logits - m)
            denom = jnp.sum(p, axis=-1, keepdims=True)
            out = jnp.dot(p.astype(BF16), v, preferred_element_type=F32) / denom
            o_ref[0] = out.astype(o_ref.dtype)


def dsa_prompt(q_ix, kw, q, k, v, n_idx, bq=256):
    b, t, w = q.shape
    h = w // HEAD_DIM
    bq = min(bq, t)
    n_sel = min(TOPK_MAX, t // 4)
    n_pair = n_idx // 2
    cchunk = min(256, t)
    wstep = max(cchunk, t // 8)
    assert t % wstep == 0 and wstep % cchunk == 0 and n_sel <= wstep
    kern = functools.partial(
        _dsa_prompt_kernel, bq=bq, s_len=t, n_sel=n_sel, n_pair=n_pair, cchunk=cchunk,
        wstep=wstep, idx_scale=(D_IDX ** -0.5) * (n_idx ** -0.5), scale=HEAD_DIM ** -0.5)
    return pl.pallas_call(
        kern,
        grid=(b, t // bq, h),
        in_specs=[pl.BlockSpec((1, bq, n_idx * D_IDX), lambda bi, qi, hi: (bi, qi, 0)),
                  pl.BlockSpec((1, bq, LANES), lambda bi, qi, hi: (bi, qi, 0)),
                  pl.BlockSpec((1, t, LANES), lambda bi, qi, hi: (bi, 0, 0)),
                  pl.BlockSpec((1, bq, HEAD_DIM), lambda bi, qi, hi: (bi, qi, hi)),
                  pl.BlockSpec((1, t, HEAD_DIM), lambda bi, qi, hi: (bi, 0, hi)),
                  pl.BlockSpec((1, t, HEAD_DIM), lambda bi, qi, hi: (bi, 0, hi))],
        out_specs=pl.BlockSpec((1, bq, HEAD_DIM), lambda bi, qi, hi: (bi, qi, hi)),
        out_shape=jax.ShapeDtypeStruct((b, t, w), BF16),
        scratch_shapes=[pltpu.VMEM((n_pair * bq, LANES), BF16),
                        pltpu.VMEM((2 * n_pair * bq, LANES), F32),
                        pltpu.VMEM((t, LANES), BF16),
                        pltpu.VMEM((t, LANES), BF16),
                        pltpu.VMEM((bq, t), jnp.int32),
                        pltpu.VMEM((bq, t), F32)],
        compiler_params=_params("parallel", "parallel", "arbitrary"),
        name="dsa_prompt",
    )(q_ix, kw, kw, q, k, v)


def _head_group(n_heads):
    return 4 if n_heads % 4 == 0 else (2 if n_heads % 2 == 0 else 1)


def _regroup_rows(ref, slab_ref, n_heads, page):
    g = slab_ref.shape[0]
    for r in range(g):
        slab_ref[r] = ref[0, pl.ds(r, page * n_heads // g, stride=g), :]


def _head_rows(slab_ref, hh, n_heads, page):
    g = slab_ref.shape[0]
    return slab_ref[hh % g, pl.ds(hh // g, page, stride=n_heads // g), :].astype(BF16)


def _head_logits(q_ref, k_ref, slab_ref, n_t, n_heads, page):
    _regroup_rows(k_ref, slab_ref, n_heads, page)
    zs = []
    for hh in range(n_heads):
        z = lax.dot_general(q_ref[0, hh], _head_rows(slab_ref, hh, n_heads, page), NT_DIMS,
                            preferred_element_type=F32)
        zs.append(z[:n_t])
    return jnp.concatenate(zs, axis=0)


def _head_weighted_values(w, v_ref, slab_ref, n_t, n_heads, page, q_rows):
    _regroup_rows(v_ref, slab_ref, n_heads, page)
    outs = []
    pad = jnp.zeros((q_rows - n_t, page), F32)
    for hh in range(n_heads):
        w_h = jnp.concatenate([w[hh * n_t:(hh + 1) * n_t], pad], axis=0).astype(BF16)
        o = jnp.dot(w_h, _head_rows(slab_ref, hh, n_heads, page), preferred_element_type=F32)
        outs.append(o[:n_t])
    return outs


def _sb_sample_kernel(pt_ref, live_in_ref, q_ref, kn_ref, vn_ref, kp_ref, vp_ref, acc_in_ref,
                      after_in_ref, acc_ref, after_ref, live_ref, slab_ref,
                      *, with_new, n_t, n_heads, page, scale):
    bi = pl.program_id(0)
    p = pl.program_id(1)
    q_rows = q_ref.shape[2]

    def step(k_ref, v_ref, is_new):
        z = _head_logits(q_ref, k_ref, slab_ref, n_t, n_heads, page) * scale
        allowed = None
        if is_new:
            row = lax.broadcasted_iota(jnp.int32, z.shape, 0)
            col = lax.broadcasted_iota(jnp.int32, z.shape, 1)
            allowed = col < (row % n_t)
        w, keep_sum = _stick_break_block(z, allowed, after_ref[0])
        for hh, o in enumerate(
                _head_weighted_values(w, v_ref, slab_ref, n_t, n_heads, page, q_rows)):
            acc_ref[0, hh] += o
        after = after_ref[0] + keep_sum
        after_ref[0] = after
        live_ref[0] = (jnp.max(after) > EXP_UNDERFLOW).astype(jnp.int32)

    @pl.when(p == 0)
    def _():
        acc_ref[...] = acc_in_ref[...]
        after_ref[...] = after_in_ref[...]
        live_ref[0] = live_in_ref[bi]

    if with_new:
        @pl.when(p == 0)
        def _():
            step(kn_ref, vn_ref, True)

        @pl.when(jnp.logical_and(p > 0, live_ref[0] > 0))
        def _():
            step(kp_ref, vp_ref, False)
    else:
        @pl.when(live_ref[0] > 0)
        def _():
            step(kp_ref, vp_ref, False)


def sb_sample_phase(page_table, live, state, q_heads, k_new, v_new, cache_k, cache_v, layer, n_t,
                    first_rank, n_page_steps, with_new):
    db, n_heads, q_rows, _ = q_heads.shape
    n_pages = page_table.shape[1]
    rows = cache_k.shape[2]
    page = rows // n_heads
    acc_in, after_in = state

    def page_map(bi, pi, pt, lv):
        step = jnp.maximum(pi - 1, 0) if with_new else pi
        rank = first_rank + jnp.where(lv[bi] > 0, step, 0)
        return (layer, pt[bi, n_pages - 1 - rank], 0, 0)

    const = lambda nd: (lambda bi, pi, pt, lv: (bi,) + (0,) * (nd - 1))
    state_specs = [pl.BlockSpec((1, n_heads, n_t, HEAD_DIM), const(4)),
                   pl.BlockSpec((1, n_heads * n_t, 1), const(3))]
    grid_spec = pltpu.PrefetchScalarGridSpec(
        num_scalar_prefetch=2,
        grid=(db, n_page_steps + (1 if with_new else 0)),
        in_specs=[pl.BlockSpec((1, n_heads, q_rows, HEAD_DIM), const(4)),
                  pl.BlockSpec((1, rows, HEAD_DIM), const(3)),
                  pl.BlockSpec((1, rows, HEAD_DIM), const(3)),
                  pl.BlockSpec((None, 1, rows, HEAD_DIM), page_map),
                  pl.BlockSpec((None, 1, rows, HEAD_DIM), page_map),
                  *state_specs],
        out_specs=state_specs,
        scratch_shapes=[pltpu.SMEM((1,), jnp.int32),
                        pltpu.VMEM((_head_group(n_heads), rows // _head_group(n_heads), HEAD_DIM),
                                   F32)],
    )
    return pl.pallas_call(
        functools.partial(_sb_sample_kernel, with_new=with_new, n_t=n_t, n_heads=n_heads,
                          page=page, scale=HEAD_DIM ** -0.5),
        grid_spec=grid_spec,
        out_shape=(jax.ShapeDtypeStruct(acc_in.shape, F32),
                   jax.ShapeDtypeStruct(after_in.shape, F32)),
        compiler_params=_params("parallel", "arbitrary"),
        name="sb_sample_new" if with_new else "sb_sample_old",
    )(page_table, live, q_heads, k_new, v_new, cache_k, cache_v, acc_in, after_in)


def sb_sample(page_table, q_heads, k_new, v_new, cache_k, cache_v, layer, n_t, first_pages=3):
    db, n_heads, _, _ = q_heads.shape
    n_pages = page_table.shape[1]
    first_pages = min(first_pages, n_pages - 1)
    state = (jnp.zeros((db, n_heads, n_t, HEAD_DIM), F32), jnp.zeros((db, n_heads * n_t, 1), F32))
    args = (q_heads, k_new, v_new, cache_k, cache_v, layer, n_t)
    state = sb_sample_phase(page_table, jnp.ones((db,), jnp.int32), state, *args,
                            first_rank=0, n_page_steps=first_pages, with_new=True)
    live = (jnp.max(state[1], axis=(1, 2)) > EXP_UNDERFLOW).astype(jnp.int32)
    acc, _ = sb_sample_phase(page_table, live, state, *args, first_rank=first_pages,
                             n_page_steps=n_pages - first_pages, with_new=False)
    return acc.transpose(0, 2, 1, 3).reshape(db, n_t, n_heads * HEAD_DIM)


def _idx_sample_kernel(pt_ref, q_ref, wb_ref, kn_ref, *refs, n_t, n_idx, page, idx_scale):
    kp_refs, o_ref = refs[:-1], refs[-1]
    p = pl.program_id(1)
    last = pl.num_programs(1) - 1

    def score(k32):
        rel = lax.dot_general(q_ref[0], k32.astype(BF16), NT_DIMS, preferred_element_type=F32)
        rel = jnp.maximum(rel, 0.0) * wb_ref[0]
        return jnp.sum(rel.reshape(n_idx, n_t, rel.shape[1]), axis=0) * idx_scale

    @pl.when(p < last)
    def _():
        for g, kp_ref in enumerate(kp_refs):
            o_ref[0, :, g * page:(g + 1) * page] = score(kp_ref[0])

    @pl.when(p == last)
    def _():
        sc = score(kn_ref[0])
        row = lax.broadcasted_iota(jnp.int32, sc.shape, 0)
        col = lax.broadcasted_iota(jnp.int32, sc.shape, 1)
        o_ref[0] = jnp.full(o_ref.shape[1:], -jnp.inf, F32)
        o_ref[0, :, :page] = jnp.where(col <= row, sc, -jnp.inf)


def idx_sample(page_table, q_rows, w_rows, k_new, cache_ik, layer, n_t, n_idx):
    db, rows, _ = q_rows.shape
    n_pages = page_table.shape[1]
    page = cache_ik.shape[2]
    group = max(g for g in (8, 4, 2, 1) if n_pages % g == 0)
    n_groups = n_pages // group

    def page_map(g):
        def index(bi, pi, pt):
            return (layer, pt[bi, jnp.minimum(pi, n_groups - 1) * group + g], 0, 0)
        return index

    grid_spec = pltpu.PrefetchScalarGridSpec(
        num_scalar_prefetch=1,
        grid=(db, n_groups + 1),
        in_specs=[pl.BlockSpec((1, rows, D_IDX), lambda bi, pi, pt: (bi, 0, 0)),
                  pl.BlockSpec((1, rows, page), lambda bi, pi, pt: (bi, 0, 0)),
                  pl.BlockSpec((1, page, D_IDX), lambda bi, pi, pt: (bi, 0, 0)),
                  *[pl.BlockSpec((None, 1, page, D_IDX), page_map(g)) for g in range(group)]],
        out_specs=pl.BlockSpec((1, n_t, group * page), lambda bi, pi, pt: (bi, 0, pi)),
    )
    return pl.pallas_call(
        functools.partial(_idx_sample_kernel, n_t=n_t, n_idx=n_idx, page=page,
                          idx_scale=(D_IDX ** -0.5) * (n_idx ** -0.5)),
        grid_spec=grid_spec,
        out_shape=jax.ShapeDtypeStruct((db, n_t, (n_groups + 1) * group * page), F32),
        compiler_params=_params("parallel", "arbitrary"),
        name="idx_sample",
    )(page_table, q_rows, w_rows, k_new, *([cache_ik] * group))


def _select_sample_kernel(s_ref, o_ref, key_ref, *, n_sel, n_cols):
    sc = s_ref[0]
    key_ref[...] = _sortable_key(sc)
    sel = _topk_select(key_ref, n_sel, n_cols)
    o_ref[0] = jnp.where(sel & (sc > -jnp.inf), 0.0, MASK_BIAS)


def select_sample(scores, n_sel):
    db, n_t, c = scores.shape
    return pl.pallas_call(
        functools.partial(_select_sample_kernel, n_sel=n_sel, n_cols=c),
        grid=(db,),
        in_specs=[pl.BlockSpec((1, n_t, c), lambda bi: (bi, 0, 0))],
        out_specs=pl.BlockSpec((1, n_t, c), lambda bi: (bi, 0, 0)),
        out_shape=jax.ShapeDtypeStruct((db, n_t, c), F32),
        scratch_shapes=[pltpu.VMEM((n_t, c), jnp.int32)],
        compiler_params=_params("parallel"),
        name="select_sample",
    )(scores)


def _dsa_sample_kernel(pt_ref, q_ref, bias_ref, kn_ref, vn_ref, *refs,
                       n_t, n_heads, page, group, scale):
    kp_refs, vp_refs = refs[:group], refs[group:2 * group]
    o_ref, acc_ref, m_ref, l_ref, slab_ref = refs[2 * group:]
    p = pl.program_id(1)
    q_rows = q_ref.shape[2]

    def step(k_ref, v_ref, g):
        bias = jnp.concatenate([bias_ref[0, :, g * page:(g + 1) * page]] * n_heads, axis=0)
        logits = _head_logits(q_ref, k_ref, slab_ref, n_t, n_heads, page) * scale + bias
        m_old = m_ref[...]
        m_new = jnp.maximum(m_old, jnp.max(logits, axis=-1, keepdims=True))
        alpha = jnp.exp(m_old - m_new)
        pr = jnp.where(bias == 0.0, jnp.exp(logits - m_new), 0.0)
        l_ref[...] = alpha * l_ref[...] + jnp.sum(pr, axis=-1, keepdims=True)
        for hh, o in enumerate(
                _head_weighted_values(pr, v_ref, slab_ref, n_t, n_heads, page, q_rows)):
            acc_ref[hh] = alpha[hh * n_t:(hh + 1) * n_t] * acc_ref[hh] + o
        m_ref[...] = m_new

    @pl.when(p == 0)
    def _():
        acc_ref[...] = jnp.zeros_like(acc_ref)
        l_ref[...] = jnp.zeros_like(l_ref)
        m_ref[...] = jnp.full_like(m_ref, MASK_BIAS)
        step(kn_ref, vn_ref, 0)

    @pl.when(p > 0)
    def _():
        for g in range(group):
            step(kp_refs[g], vp_refs[g], g)

    @pl.when(p == pl.num_programs(1) - 1)
    def _():
        inv = 1.0 / l_ref[...]
        for hh in range(n_heads):
            o_ref[0, :, hh * HEAD_DIM:(hh + 1) * HEAD_DIM] = (
                acc_ref[hh] * inv[hh * n_t:(hh + 1) * n_t])


def dsa_sample(page_table, q_heads, bias, k_new, v_new, cache_k, cache_v, layer, n_t):
    db, n_heads, q_rows, _ = q_heads.shape
    n_pages = page_table.shape[1]
    rows = cache_k.shape[2]
    page = rows // n_heads
    w = n_heads * HEAD_DIM
    group = max(g for g in (4, 2, 1) if n_pages % g == 0)
    n_groups = n_pages // group
    assert bias.shape[2] % (group * page) == 0

    def page_map(g):
        def index(bi, pi, pt):
            return (layer, pt[bi, (jnp.maximum(pi, 1) - 1) * group + g], 0, 0)
        return index

    def bias_map(bi, pi, pt):
        return (bi, 0, jnp.where(pi == 0, n_groups, pi - 1))

    page_specs = [pl.BlockSpec((None, 1, rows, HEAD_DIM), page_map(g)) for g in range(group)]
    grid_spec = pltpu.PrefetchScalarGridSpec(
        num_scalar_prefetch=1,
        grid=(db, n_groups + 1),
        in_specs=[pl.BlockSpec((1, n_heads, q_rows, HEAD_DIM), lambda bi, pi, pt: (bi, 0, 0, 0)),
                  pl.BlockSpec((1, n_t, group * page), bias_map),
                  pl.BlockSpec((1, rows, HEAD_DIM), lambda bi, pi, pt: (bi, 0, 0)),
                  pl.BlockSpec((1, rows, HEAD_DIM), lambda bi, pi, pt: (bi, 0, 0)),
                  *page_specs, *page_specs],
        out_specs=pl.BlockSpec((1, n_t, w), lambda bi, pi, pt: (bi, 0, 0)),
        scratch_shapes=[pltpu.VMEM((n_heads, n_t, HEAD_DIM), F32),
                        pltpu.VMEM((n_heads * n_t, 1), F32),
                        pltpu.VMEM((n_heads * n_t, 1), F32),
                        pltpu.VMEM((_head_group(n_heads), rows // _head_group(n_heads), HEAD_DIM),
                                   F32)],
    )
    return pl.pallas_call(
        functools.partial(_dsa_sample_kernel, n_t=n_t, n_heads=n_heads, page=page, group=group,
                          scale=HEAD_DIM ** -0.5),
        grid_spec=grid_spec,
        out_shape=jax.ShapeDtypeStruct((db, n_t, w), F32),
        compiler_params=_params("parallel", "arbitrary"),
        name="dsa_sample",
    )(page_table, q_heads, bias, k_new, v_new, *([cache_k] * group), *([cache_v] * group))


def _rope_tables(pos, d, reps_rows):
    inv = ROPE_THETA ** (-jnp.arange(0, d, 2, dtype=F32) / d)
    ang = pos.astype(F32)[:, None] * inv[None, :]
    cos = jnp.cos(ang)
    sin = jnp.sin(ang)
    cosf = jnp.tile(jnp.concatenate([cos, cos], axis=-1), (reps_rows, LANES // d))
    sinf = jnp.tile(jnp.concatenate([-sin, sin], axis=-1), (reps_rows, LANES // d))
    return cosf, sinf


def _head_major_queries(q, n_heads, q_rows):
    db, n_t, _ = q.shape
    q4 = q.reshape(db, n_t, n_heads, HEAD_DIM).transpose(0, 2, 1, 3)
    return jnp.pad(q4, ((0, 0), (0, 0), (0, q_rows - n_t), (0, 0)))


def _new_rows_as_page(x, n_heads, page):
    db, n_t, _ = x.shape
    rows = x.reshape(db, n_t * n_heads, HEAD_DIM)
    return jnp.pad(rows, ((0, 0), (0, (page - n_t) * n_heads), (0, 0)))


def _pad_rows(x, rows):
    return jnp.pad(x, ((0, 0), (0, rows - x.shape[1]), (0, 0)))


def kernel(x_prompt, x_sample, cache_sb_k, cache_sb_v, cache_dsa_k, cache_dsa_v, cache_idx_k,
           page_table, norm_mix, w_in, w_branch_sb, w_branch_dsa, w_out, norm_mlp, w_up, w_down,
           norm_final):
    bsz, seq, d_model = x_prompt.shape
    db, n_t, _ = x_sample.shape
    depth, n_pool, page, h_sb, _ = cache_sb_k.shape
    h_dsa = cache_dsa_k.shape[3]
    n_pages = page_table.shape[1]
    past_len = n_pages * page
    sb_w = h_sb * HEAD_DIM
    dsa_w = h_dsa * HEAD_DIM
    n_in = w_in.shape[2]
    n_idx = (n_in - 3 * sb_w - 3 * dsa_w - D_IDX - 2 * d_model) // (D_IDX + 1)
    idx_w = n_idx * D_IDX
    c_qsb, c_ksb, c_vsb = 0, sb_w, 2 * sb_w
    c_qds, c_kds, c_vds = 3 * sb_w, 3 * sb_w + dsa_w, 3 * sb_w + 2 * dsa_w
    c_qix = 3 * sb_w + 3 * dsa_w
    c_kw = c_qix + idx_w
    c_gate = c_kw + D_IDX + n_idx
    assert D_IDX + n_idx <= LANES and n_t <= page

    w_main = w_in[:, :, :c_kw].astype(BF16)
    w_kw = jnp.pad(w_in[:, :, c_kw:c_gate], ((0, 0), (0, 0), (0, LANES - (c_gate - c_kw)))).astype(BF16)
    w_gate = w_in[:, :, c_gate:].astype(BF16)
    w_bsb = w_branch_sb.astype(BF16)
    w_bds = w_branch_dsa.astype(BF16)
    w_o = column_blocks(w_out.astype(BF16), PROJ_TN)
    w_u = w_up.astype(BF16)
    w_d = w_down.astype(BF16)

    pos_p = jnp.arange(seq, dtype=jnp.int32)
    pos_s = past_len + jnp.arange(n_t, dtype=jnp.int32)
    tables = {}
    for name, pos, reps in (("p", pos_p, bsz), ("s", pos_s, db)):
        cos128, sin128 = _rope_tables(pos, HEAD_DIM, reps)
        cos64, sin64 = _rope_tables(pos, D_IDX, reps)
        lane = jnp.arange(LANES)[None, :]
        cos_kw = jnp.where(lane < D_IDX, cos64, 1.0)
        sin_kw = jnp.where(lane < D_IDX, sin64, 0.0)
        tables[name] = (cos128, sin128, cos64, sin64, cos_kw, sin_kw)

    cache_sb_k2 = cache_sb_k.reshape(depth, n_pool, page * h_sb, HEAD_DIM)
    cache_sb_v2 = cache_sb_v.reshape(depth, n_pool, page * h_sb, HEAD_DIM)
    cache_ds_k2 = cache_dsa_k.reshape(depth, n_pool, page * h_dsa, HEAD_DIM)
    cache_ds_v2 = cache_dsa_v.reshape(depth, n_pool, page * h_dsa, HEAD_DIM)
    q_rows_pad = -(-n_t // BF16_ROWS) * BF16_ROWS

    def project(xn, layer, tbl):
        cos128, sin128, cos64, sin64, cos_kw, sin_kw = tbl
        m = xn.shape[0]
        tm = min(1024, m)
        rope_specs = (_row_spec(tm, LANES), _row_spec(tm, LANES))
        mm = functools.partial(matmul, xn, tm=tm, tk=d_model)
        q_sb = mm(w_main, layer, c_qsb, sb_w, BF16)
        k_sb, k_sb16 = mm(w_main, layer, c_ksb, sb_w, F32, copy_dtype=BF16)
        v_sb, v_sb16 = mm(w_main, layer, c_vsb, sb_w, F32, copy_dtype=BF16)
        q_ds = mm(w_main, layer, c_qds, dsa_w, BF16, _ep_rope128, (cos128, sin128), rope_specs)
        k_ds, k_ds16 = mm(w_main, layer, c_kds, dsa_w, F32, _ep_rope128, (cos128, sin128),
                          rope_specs, copy_dtype=BF16)
        v_ds, v_ds16 = mm(w_main, layer, c_vds, dsa_w, F32, copy_dtype=BF16)
        q_ix = mm(w_main, layer, c_qix, idx_w, BF16, _ep_rope64, (cos64, sin64), rope_specs)
        kw = mm(w_kw, layer, 0, LANES, F32, _ep_rope64, (cos_kw, sin_kw), rope_specs)
        gates = mm(w_gate, layer, 0, 2 * d_model, BF16, _ep_sigmoid)
        kv16 = (k_sb16, v_sb16, k_ds16, v_ds16)
        return q_sb, k_sb, v_sb, q_ds, k_ds, v_ds, q_ix, kw, gates, kv16

    def finish_layer(h, o_sb, o_ds, gates, layer):
        m = h.shape[0]
        tm = min(1024, m)
        merged = merge_branches(o_sb, o_ds, w_bsb, w_bds, gates, layer, d_model, tm=tm)
        tm_proj = min(PROJ_TM, m)
        h = matmul(merged, w_o, layer, 0, d_model, F32, _ep_residual, (h,),
                   (_tile_spec(tm_proj, PROJ_TN),), tm=tm_proj, tk=d_model)
        xn = rmsnorm(h, norm_mlp[layer], BF16)
        hid = matmul(xn, w_u, layer, 0, w_up.shape[2], BF16, _ep_relu2, tm=tm, tk=d_model)
        tn_down = min(DOWN_TN, d_model)
        h = matmul(hid, w_d, layer, 0, d_model, F32, _ep_residual, (h,), (_tile_spec(tm, tn_down),),
                   tm=tm, tn=tn_down, tk=2048)
        return h

    hp = x_prompt.reshape(bsz * seq, d_model)
    hs = x_sample.reshape(db * n_t, d_model)
    rows_p = [[] for _ in range(5)]
    rows_s = [[] for _ in range(5)]
    n_sel_s = min(TOPK_MAX, (past_len + n_t) // 4)

    for layer in range(depth):
        xn = rmsnorm(hp, norm_mix[layer], BF16)
        q_sb, k_sb, v_sb, q_ds, k_ds, v_ds, q_ix, kw, gates, kv16 = project(
            xn, layer, tables["p"])
        r3 = lambda a: a.reshape(bsz, seq, a.shape[1])
        o_sb = sb_prompt(r3(q_sb), r3(kv16[0]), r3(kv16[1]))
        o_ds = dsa_prompt(r3(q_ix), r3(kw), r3(q_ds), r3(kv16[2]), r3(kv16[3]), n_idx)
        hp = finish_layer(hp, o_sb.reshape(bsz * seq, sb_w), o_ds.reshape(bsz * seq, dsa_w),
                          gates, layer)
        for lst, r in zip(rows_p, (k_sb, v_sb, k_ds, v_ds, kw[:, :D_IDX])):
            lst.append(r)

        xn = rmsnorm(hs, norm_mix[layer], BF16)
        q_sb, k_sb, v_sb, q_ds, k_ds, v_ds, q_ix, kw, gates, _ = project(xn, layer, tables["s"])
        s3 = lambda a: a.reshape(db, n_t, a.shape[1])
        o_sb = sb_sample(page_table, _head_major_queries(s3(q_sb), h_sb, q_rows_pad),
                         _new_rows_as_page(s3(k_sb), h_sb, page),
                         _new_rows_as_page(s3(v_sb), h_sb, page),
                         cache_sb_k2, cache_sb_v2, layer, n_t)
        kw3 = s3(kw)
        q_rows = s3(q_ix).reshape(db, n_t, n_idx, D_IDX).transpose(0, 2, 1, 3).reshape(
            db, n_idx * n_t, D_IDX)
        w_rows = jnp.broadcast_to(
            kw3[:, :, D_IDX:D_IDX + n_idx].transpose(0, 2, 1).reshape(db, n_idx * n_t, 1),
            (db, n_idx * n_t, page))
        scores = idx_sample(page_table, q_rows, w_rows, _pad_rows(kw3[:, :, :D_IDX], page),
                            cache_idx_k, layer, n_t, n_idx)
        bias = select_sample(scores, n_sel_s)
        o_ds = dsa_sample(page_table, _head_major_queries(s3(q_ds), h_dsa, q_rows_pad), bias,
                          _new_rows_as_page(s3(k_ds), h_dsa, page),
                          _new_rows_as_page(s3(v_ds), h_dsa, page),
                          cache_ds_k2, cache_ds_v2, layer, n_t)
        hs = finish_layer(hs, o_sb.reshape(db * n_t, sb_w).astype(BF16),
                          o_ds.reshape(db * n_t, dsa_w).astype(BF16), gates, layer)
        for lst, r in zip(rows_s, (k_sb, v_sb, k_ds, v_ds, kw[:, :D_IDX])):
            lst.append(r)

    y_prompt = rmsnorm(hp, norm_final, F32).reshape(bsz, seq, d_model)
    y_sample = rmsnorm(hs, norm_final, F32).reshape(db, n_t, d_model)

    def stack(lst, lead, heads):
        a = jnp.stack(lst, axis=0)
        if heads is None:
            return a.reshape((depth,) + lead + (D_IDX,))
        return a.reshape((depth,) + lead + (heads, HEAD_DIM))

    outs_p = [stack(rows_p[0], (bsz, seq), h_sb), stack(rows_p[1], (bsz, seq), h_sb),
              stack(rows_p[2], (bsz, seq), h_dsa), stack(rows_p[3], (bsz, seq), h_dsa),
              stack(rows_p[4], (bsz, seq), None)]
    outs_s = [stack(rows_s[0], (db, n_t), h_sb), stack(rows_s[1], (db, n_t), h_sb),
              stack(rows_s[2], (db, n_t), h_dsa), stack(rows_s[3], (db, n_t), h_dsa),
              stack(rows_s[4], (db, n_t), None)]
    return (y_prompt, y_sample, *outs_p, *outs_s)
```
